```python
import math
import jax
import jax.numpy as jnp
from jax import lax
import numpy as np

D_MODEL = 4096
BATCH = 4
SEQ = 2048
DEPTH = 2
DEC_BATCH = 8
DEC_SEQ = 1
PAST_LEN = 16384
PAGE_SIZE = 128

HEAD_DIM = 128
D_A = D_MODEL // 4
D_B = D_MODEL // 4
D_C = D_MODEL // 4
D_D = D_MODEL // 4
D_MIX = D_A + D_B + D_C + D_D
D_IN = 2 * D_A + 3 * D_B + 2 * D_C + 3 * D_D
H_A = D_A // HEAD_DIM
H_B = D_B // HEAD_DIM
CHUNK = 128
BRANCHES = ((128, 1), (512, 4), (2048, 16))
WIN_MAX = 2048
BAND_BLK = 128
CONV_C = 31
CONV_D = 3
D_FF = 11008
N_EXPERTS = 8
TOP_K = 2
D_FF_EXPERT = 14336
N_DENSE = (DEPTH + 1) // 2
N_MOE = DEPTH // 2
ALPHA = (2.0 * DEPTH) ** 0.25
BETA = (8.0 * DEPTH) ** -0.25
LN_EPS = 1e-5
ATTN_SCALE = HEAD_DIM ** -0.5

kernel_name = "hybrid_gmlp_dilattn_conformer_shortconv_deepnorm_step"


def layer_norm(x, g, b):
    xf = x.astype(jnp.float32)
    mu = jnp.mean(xf, axis=-1, keepdims=True)
    var = jnp.mean(jnp.square(xf - mu), axis=-1, keepdims=True)
    y = (xf - mu) * lax.rsqrt(var + LN_EPS) * g.astype(jnp.float32) + b.astype(jnp.float32)
    return y.astype(x.dtype)


def alibi_slopes(n_heads):
    return jnp.asarray([2.0 ** (-8.0 * (h + 1) / n_heads) for h in range(n_heads)], dtype=jnp.float32)


def split_projection(z):
    sizes = (2 * D_A, 3 * D_B, 2 * D_C, 3 * D_D)
    parts, start = [], 0
    for s in sizes:
        parts.append(z[..., start:start + s])
        start += s
    return parts


def causal_dwconv(x_ext, w):
    c = x_ext.shape[-1]
    return lax.conv_general_dilated(x_ext, w[:, None, :].astype(x_ext.dtype), window_strides=(1,),
                                    padding="VALID", dimension_numbers=("NWC", "WIO", "NWC"),
                                    feature_group_count=c)


def mixer_a(za, ln_g, ln_b, w_s, b_s, n_pos):
    b, t, _ = za.shape
    u, v = jnp.split(jax.nn.gelu(za), 2, axis=-1)
    v = layer_norm(v, ln_g, ln_b)
    vc = v.reshape(b, t // n_pos, n_pos, H_A, D_A // H_A)
    ws = jnp.tril(w_s[:, :n_pos, :n_pos])
    mixed = jnp.einsum("hts,bcshd->bcthd", ws.astype(vc.dtype), vc)
    mixed = mixed + b_s[:, :n_pos].T[None, None, :, :, None].astype(vc.dtype)
    return u * mixed.reshape(b, t, D_A), v


def mixer_c(zc, past, conv_w, conv_b, ln_g, ln_b):
    a, g = jnp.split(zc, 2, axis=-1)
    h = a * jax.nn.sigmoid(g)
    h_ext = jnp.concatenate([past.astype(h.dtype), h], axis=1)
    y = causal_dwconv(h_ext, conv_w) + conv_b.astype(h.dtype)
    y = jax.nn.silu(layer_norm(y, ln_g, ln_b))
    return y, h_ext[:, h_ext.shape[1] - (CONV_C - 1):]


def mixer_d(zd, past, conv_w):
    bg, cg, h = jnp.split(zd, 3, axis=-1)
    ch = cg * h
    ch_ext = jnp.concatenate([past.astype(ch.dtype), ch], axis=1)
    y = bg * causal_dwconv(ch_ext, conv_w)
    return y, ch_ext[:, ch_ext.shape[1] - (CONV_D - 1):]


def qkv_heads(zb):
    b, t, _ = zb.shape
    q, k, v = jnp.split(zb, 3, axis=-1)
    return (q.reshape(b, t, H_B, HEAD_DIM), k.reshape(b, t, H_B, HEAD_DIM),
            v.reshape(b, t, H_B, HEAD_DIM))


def band_branch_prompt(q, k, v, n_back, dil, slopes):
    b, s, h, dh = q.shape
    L = s // dil
    nb = -(-L // BAND_BLK)
    Lp = nb * BAND_BLK

    def to_sub(t):
        t = t.reshape(b, L, dil, h, dh).transpose(0, 2, 1, 3, 4).reshape(b * dil, L, h, dh)
        return jnp.pad(t, ((0, 0), (0, Lp - L), (0, 0), (0, 0)))

    def band(t):
        tb = t.reshape(b * dil, nb, BAND_BLK, h, dh)
        prev = jnp.pad(tb, ((0, 0), (1, 0), (0, 0), (0, 0), (0, 0)))[:, :-1]
        return jnp.concatenate([prev, tb], axis=2)

    qb = to_sub(q).reshape(b * dil, nb, BAND_BLK, h, dh)
    kb, vb = band(to_sub(k)), band(to_sub(v))
    qq = jnp.arange(BAND_BLK)[:, None]
    kj = jnp.arange(2 * BAND_BLK)[None, :]
    j = qq + BAND_BLK - kj
    key_idx = jnp.arange(nb)[:, None, None] * BAND_BLK - BAND_BLK + kj
    valid = (j >= 0) & (j <= n_back) & (key_idx >= 0)
    sc = jnp.einsum("bnqhd,bnkhd->bnhqk", qb, kb, preferred_element_type=jnp.float32) * ATTN_SCALE
    sc = sc - slopes[:, None, None] * (j * dil).astype(jnp.float32)[None]
    sc = jnp.where(valid[None, :, None], sc, -jnp.inf)
    m = jnp.max(sc, axis=-1, keepdims=True)
    p = jnp.exp(sc - m)
    den = jnp.sum(p, axis=-1)
    o = jnp.einsum("bnhqk,bnkhd->bnqhd", p, vb.astype(jnp.float32))
    o = o / den.transpose(0, 1, 3, 2)[..., None]
    lse = (m[..., 0] + jnp.log(den)).transpose(0, 1, 3, 2)
    o = o.reshape(b * dil, Lp, h, dh)[:, :L].reshape(b, dil, L, h, dh)
    o = o.transpose(0, 2, 1, 3, 4).reshape(b, s, h, dh)
    lse = lse.reshape(b * dil, Lp, h)[:, :L].reshape(b, dil, L, h).transpose(0, 2, 1, 3).reshape(b, s, h)
    return o, lse


def combine_branches(outs, lses):
    w = jax.nn.softmax(jnp.stack(lses, axis=0), axis=0)
    return jnp.sum(w[..., None] * jnp.stack(outs, axis=0), axis=0)


def dilated_attention_prompt(q, k, v, slopes):
    outs, lses = [], []
    for window, dil in BRANCHES:
        o, lse = band_branch_prompt(q, k, v, window // dil, dil, slopes)
        outs.append(o)
        lses.append(lse)
    return combine_branches(outs, lses)


def dilated_attention_sample(q, k_all, v_all, win, slopes):
    t = q.shape[1]
    outs, lses = [], []
    for window, dil in BRANCHES:
        n_back = window // dil
        i = jnp.arange(t)[:, None]
        j = jnp.arange(n_back + 1)[None, :]
        idx = win + i - j * dil
        valid = idx >= 0
        idx = jnp.clip(idx, 0, win + t - 1)
        kg = k_all[:, idx]
        vg = v_all[:, idx]
        sc = jnp.einsum("bqhd,bqkhd->bhqk", q, kg, preferred_element_type=jnp.float32) * ATTN_SCALE
        sc = sc - slopes[:, None, None] * (j * dil).astype(jnp.float32)
        sc = jnp.where(valid[None, None], sc, -jnp.inf)
        m = jnp.max(sc, axis=-1, keepdims=True)
        p = jnp.exp(sc - m)
        den = jnp.sum(p, axis=-1)
        o = jnp.einsum("bhqk,bqkhd->bqhd", p, vg.astype(jnp.float32))
        o = o / den.transpose(0, 2, 1)[..., None]
        outs.append(o)
        lses.append((m[..., 0] + jnp.log(den)).transpose(0, 2, 1))
    return combine_branches(outs, lses)


def swiglu(x, w1, w3, w2):
    return (jax.nn.silu(x @ w1) * (x @ w3)) @ w2


def moe_swiglu(x, router, w1, w3, w2):
    shape = x.shape
    xt = x.reshape(-1, shape[-1])
    logits = (xt @ router).astype(jnp.float32)
    top_v, top_i = lax.top_k(logits, TOP_K)
    gates = jax.nn.softmax(top_v, axis=-1)
    gate_e = jnp.sum(jax.nn.one_hot(top_i, N_EXPERTS, dtype=jnp.float32) * gates[..., None], axis=1)
    out = jnp.zeros(xt.shape, jnp.float32)
    for e in range(N_EXPERTS):
        out = out + gate_e[:, e:e + 1] * swiglu(xt, w1[e], w3[e], w2[e]).astype(jnp.float32)
    return out.astype(x.dtype).reshape(shape)


def setup_inputs(seed: int = 0) -> dict:
    key = jax.random.key(seed)
    keys = jax.random.split(key, 32)
    cnt = [0]

    def nrm(shape, scale):
        k = keys[cnt[0]]
        cnt[0] += 1
        return jax.random.normal(k, shape, jnp.float32) * scale

    win = min(WIN_MAX, PAST_LEN)
    return {
        "x_prompt": nrm((BATCH, SEQ, D_MODEL), 1.0),
        "x_sample": nrm((DEC_BATCH, DEC_SEQ, D_MODEL), 1.0),
        "cache_b_k": nrm((DEPTH, DEC_BATCH, win, H_B, HEAD_DIM), 1.0),
        "cache_b_v": nrm((DEPTH, DEC_BATCH, win, H_B, HEAD_DIM), 1.0),
        "state_c": nrm((DEPTH, DEC_BATCH, CONV_C - 1, D_C), 0.5),
        "state_d": nrm((DEPTH, DEC_BATCH, CONV_D - 1, D_D), 0.5),
        "w_in": nrm((DEPTH, D_MODEL, D_IN), D_MODEL ** -0.5),
        "a_ln_g": 1.0 + nrm((DEPTH, D_A), 0.01),
        "a_ln_b": nrm((DEPTH, D_A), 0.01),
        "a_ws": nrm((DEPTH, H_A, CHUNK, CHUNK), CHUNK ** -0.5),
        "a_bs": 1.0 + nrm((DEPTH, H_A, CHUNK), 0.01),
        "c_conv_w": nrm((DEPTH, CONV_C, D_C), CONV_C ** -0.5),
        "c_conv_b": nrm((DEPTH, D_C), 0.01),
        "c_ln_g": 1.0 + nrm((DEPTH, D_C), 0.01),
        "c_ln_b": nrm((DEPTH, D_C), 0.01),
        "d_conv_w": nrm((DEPTH, CONV_D, D_D), CONV_D ** -0.5),
        "w_out": nrm((DEPTH, D_MIX, D_MODEL), BETA * D_MIX ** -0.5),
        "ln1_g": 1.0 + nrm((DEPTH, D_MODEL), 0.01),
        "ln1_b": nrm((DEPTH, D_MODEL), 0.01),
        "ln2_g": 1.0 + nrm((DEPTH, D_MODEL), 0.01),
        "ln2_b": nrm((DEPTH, D_MODEL), 0.01),
        "ffn_w1": nrm((N_DENSE, D_MODEL, D_FF), D_MODEL ** -0.5),
        "ffn_w3": nrm((N_DENSE, D_MODEL, D_FF), D_MODEL ** -0.5),
        "ffn_w2": nrm((N_DENSE, D_FF, D_MODEL), BETA * D_FF ** -0.5),
        "moe_router": nrm((N_MOE, D_MODEL, N_EXPERTS), D_MODEL ** -0.5),
        "moe_w1": nrm((N_MOE, N_EXPERTS, D_MODEL, D_FF_EXPERT), D_MODEL ** -0.5),
        "moe_w3": nrm((N_MOE, N_EXPERTS, D_MODEL, D_FF_EXPERT), D_MODEL ** -0.5),
        "moe_w2": nrm((N_MOE, N_EXPERTS, D_FF_EXPERT, D_MODEL), BETA * D_FF_EXPERT ** -0.5),
    }


def reference(x_prompt, x_sample, cache_b_k, cache_b_v, state_c, state_d,
              w_in, a_ln_g, a_ln_b, a_ws, a_bs, c_conv_w, c_conv_b, c_ln_g, c_ln_b,
              d_conv_w, w_out, ln1_g, ln1_b, ln2_g, ln2_b,
              ffn_w1, ffn_w3, ffn_w2, moe_router, moe_w1, moe_w3, moe_w2):
    slopes = alibi_slopes(H_B)
    xp, xs = x_prompt, x_sample
    bp, sp = xp.shape[0], xp.shape[1]
    ts = xs.shape[1]
    win = cache_b_k.shape[2]
    win_p = min(WIN_MAX, sp)
    pk, pv, sk, sv, pc, scs, pd, sds, sa = [], [], [], [], [], [], [], [], []
    for l in range(DEPTH):
        za, zb, zc, zd = split_projection(xp @ w_in[l])
        ya, _ = mixer_a(za, a_ln_g[l], a_ln_b[l], a_ws[l], a_bs[l], CHUNK)
        q, k, v = qkv_heads(zb)
        yb = dilated_attention_prompt(q, k, v, slopes).astype(xp.dtype).reshape(bp, sp, D_B)
        yc, c_buf = mixer_c(zc, jnp.zeros((bp, CONV_C - 1, D_C), zc.dtype),
                            c_conv_w[l], c_conv_b[l], c_ln_g[l], c_ln_b[l])
        yd, d_buf = mixer_d(zd, jnp.zeros((bp, CONV_D - 1, D_D), zd.dtype), d_conv_w[l])
        mix_p = jnp.concatenate([ya, yb, yc, yd], axis=-1) @ w_out[l]
        pk.append(k[:, sp - win_p:])
        pv.append(v[:, sp - win_p:])
        pc.append(c_buf)
        pd.append(d_buf)

        za_s, zb_s, zc_s, zd_s = split_projection(xs @ w_in[l])
        ya_s, v_a_s = mixer_a(za_s, a_ln_g[l], a_ln_b[l], a_ws[l], a_bs[l], ts)
        q_s, k_s, v_s = qkv_heads(zb_s)
        k_all = jnp.concatenate([cache_b_k[l].astype(k_s.dtype), k_s], axis=1)
        v_all = jnp.concatenate([cache_b_v[l].astype(v_s.dtype), v_s], axis=1)
        yb_s = dilated_attention_sample(q_s, k_all, v_all, win, slopes).astype(xs.dtype)
        yb_s = yb_s.reshape(xs.shape[0], ts, D_B)
        yc_s, c_buf_s = mixer_c(zc_s, state_c[l], c_conv_w[l], c_conv_b[l], c_ln_g[l], c_ln_b[l])
        yd_s, d_buf_s = mixer_d(zd_s, state_d[l], d_conv_w[l])
        mix_s = jnp.concatenate([ya_s, yb_s, yc_s, yd_s], axis=-1) @ w_out[l]
        sk.append(k_s)
        sv.append(v_s)
        scs.append(c_buf_s)
        sds.append(d_buf_s)
        sa.append(v_a_s)

        xp = layer_norm(ALPHA * xp + mix_p, ln1_g[l], ln1_b[l])
        xs = layer_norm(ALPHA * xs + mix_s, ln1_g[l], ln1_b[l])

        i = l // 2
        if l % 2 == 0:
            fp = swiglu(xp, ffn_w1[i], ffn_w3[i], ffn_w2[i])
            fs = swiglu(xs, ffn_w1[i], ffn_w3[i], ffn_w2[i])
        else:
            fp = moe_swiglu(xp, moe_router[i], moe_w1[i], moe_w3[i], moe_w2[i])
            fs = moe_swiglu(xs, moe_router[i], moe_w1[i], moe_w3[i], moe_w2[i])
        xp = layer_norm(ALPHA * xp + fp, ln2_g[l], ln2_b[l])
        xs = layer_norm(ALPHA * xs + fs, ln2_g[l], ln2_b[l])

    return (xp, xs, jnp.stack(pk), jnp.stack(pv), jnp.stack(sk), jnp.stack(sv),
            jnp.stack(pc), jnp.stack(scs), jnp.stack(pd), jnp.stack(sds), jnp.stack(sa))
```

```python
import functools

import jax
import jax.numpy as jnp
from jax import lax
from jax.experimental import pallas as pl
from jax.experimental.pallas import tpu as pltpu

D_MODEL = 4096
BATCH = 4
SEQ = 2048
DEPTH = 2
DEC_BATCH = 8
HEAD_DIM = 128
D_Q = D_MODEL // 4
N_HEADS = D_Q // HEAD_DIM
D_IN = 10 * D_Q
CHUNK = 128
DILATIONS = (1, 4, 16)
N_BACK = 128
WIN = 2048
CONV_C = 31
CONV_D = 3
D_FF = 11008
N_EXPERTS = 8
TOP_K = 2
D_FF_EXPERT = 14336
ALPHA = (2.0 * DEPTH) ** 0.25
LN_EPS = 1e-5
ATTN_SCALE = HEAD_DIM ** -0.5
NEG = -1e30

T_PROMPT = BATCH * SEQ
T_REAL = T_PROMPT + DEC_BATCH
T_PAD = 8448
BM = 1056
BN_PROJ = 512
BT_MIX = 256
BM_LN = 256
FF_BN = 256
FF_TK = 1024
FF_BN_DOWN = 2048
BM_MOE = 1280
SB_MOE = 256
N_SLOTS = T_REAL * TOP_K
MOE_TILES = (N_SLOTS + N_EXPERTS * (BM_MOE - 1)) // BM_MOE
S_PAD = MOE_TILES * BM_MOE
R_GATHER = 256
R_COMBINE = 128
V7X_VMEM_BYTES = 64 * 1024 * 1024

F32 = jnp.float32
BF16 = jnp.bfloat16


def _params(sem, vmem_mb):
    assert vmem_mb * 1024 * 1024 < V7X_VMEM_BYTES
    return pltpu.CompilerParams(dimension_semantics=sem, vmem_limit_bytes=vmem_mb * 1024 * 1024)


def _sigmoid(x):
    return 1.0 / (1.0 + jnp.exp(-x))


def _layer_norm_rows(y, g, b):
    mu = jnp.mean(y, axis=-1, keepdims=True)
    yc = y - mu
    var = jnp.mean(yc * yc, axis=-1, keepdims=True)
    return yc * lax.rsqrt(var + LN_EPS) * g + b


def _cast_rows(dst_ref, src_ref, rows, chunk=256):
    def body(i, c):
        sl = pl.ds(pl.multiple_of(i * chunk, chunk), chunk)
        dst_ref[sl, :] = src_ref[sl, :].astype(BF16)
        return c
    lax.fori_loop(0, rows // chunk, body, 0)


def _proj_body(x_ref, w_ref, o_ref, wb_ref):
    @pl.when(pl.program_id(1) == 0)
    def _():
        _cast_rows(wb_ref, w_ref, wb_ref.shape[0])
    o_ref[...] = jnp.dot(x_ref[...], wb_ref[...], preferred_element_type=F32)


def _proj(x_bf, w_stack, layer):
    m, k = x_bf.shape
    n = w_stack.shape[-1]
    return pl.pallas_call(
        _proj_body,
        grid=(n // BN_PROJ, m // BM),
        in_specs=[pl.BlockSpec((BM, k), lambda j, i: (i, 0)),
                  pl.BlockSpec((None, k, BN_PROJ), lambda j, i: (layer, 0, j))],
        out_specs=pl.BlockSpec((BM, BN_PROJ), lambda j, i: (i, j)),
        out_shape=jax.ShapeDtypeStruct((m, n), F32),
        scratch_shapes=[pltpu.VMEM((k, BN_PROJ), BF16)],
        compiler_params=_params(("arbitrary", "arbitrary"), 48),
        name="proj",
    )(x_bf, w_stack)


def _mix_acd_body(up_ref, vp_ref, ca_ref, cg_ref, dbg_ref, dcg_ref, dh_ref,
                  lng_ref, lnb_ref, ws_ref, bst_ref, cw_ref, cb_ref, clg_ref, clb_ref, dw_ref,
                  ya_ref, yc_ref, yd_ref, sc_ref, sd_ref,
                  vbuf, hext, chext, ybuf):
    t = pl.program_id(1)
    nt = pl.num_programs(1)
    bt = BT_MIX

    @pl.when(t == 0)
    def _():
        hext[0:32, :] = jnp.zeros((32, D_Q), F32)
        chext[0:8, :] = jnp.zeros((8, D_Q), F32)

    v = jax.nn.gelu(vp_ref[...])
    vbuf[...] = _layer_norm_rows(v, lng_ref[...], lnb_ref[...]).astype(BF16)
    row = lax.broadcasted_iota(jnp.int32, (CHUNK, CHUNK), 0)
    col = lax.broadcasted_iota(jnp.int32, (CHUNK, CHUNK), 1)
    for h in range(N_HEADS):
        cs = slice(h * HEAD_DIM, (h + 1) * HEAD_DIM)
        w = jnp.where(row >= col, ws_ref[h], 0.0).astype(BF16)
        bias = bst_ref[:, h:h + 1]
        for c in range(bt // CHUNK):
            rs = slice(c * CHUNK, (c + 1) * CHUNK)
            mixed = jnp.dot(w, vbuf[rs, cs], preferred_element_type=F32) + bias
            ya_ref[rs, cs] = (jax.nn.gelu(up_ref[rs, cs]) * mixed).astype(BF16)

    hext[32:32 + bt, :] = ca_ref[...] * _sigmoid(cg_ref[...])
    rc = 64
    for cc in range(D_Q // 128):
        cs = slice(cc * 128, (cc + 1) * 128)
        for r0 in range(0, bt, rc):
            acc = jnp.zeros((rc, 128), F32) + cb_ref[:, cs]
            for k in range(CONV_C):
                acc = acc + cw_ref[k:k + 1, cs] * hext[r0 + 2 + k:r0 + 2 + k + rc, cs]
            ybuf[r0:r0 + rc, cs] = acc
    y = _layer_norm_rows(ybuf[...], clg_ref[...], clb_ref[...])
    yc_ref[...] = (y * _sigmoid(y)).astype(BF16)

    @pl.when(t == nt - 1)
    def _():
        sc_ref[...] = hext[bt + 2:bt + 32, :]
    hext[0:32, :] = hext[bt:bt + 32, :]

    chext[8:8 + bt, :] = dcg_ref[...] * dh_ref[...]
    conv = (dw_ref[0:1, :] * chext[6:6 + bt, :] + dw_ref[1:2, :] * chext[7:7 + bt, :]
            + dw_ref[2:3, :] * chext[8:8 + bt, :])
    yd_ref[...] = (dbg_ref[...] * conv).astype(BF16)

    @pl.when(t == nt - 1)
    def _():
        sd_ref[...] = chext[bt + 6:bt + 8, :]
    chext[0:8, :] = chext[bt:bt + 8, :]


def _mix_acd(z, layer, a_ln_g, a_ln_b, a_ws, a_bs_t, c_conv_w, c_conv_b, c_ln_g, c_ln_b, d_conv_w):
    nt = SEQ // BT_MIX

    def zcol(c):
        return pl.BlockSpec((BT_MIX, D_Q), lambda b, t: (b * nt + t, c))

    def vec(rows):
        return pl.BlockSpec((None, rows, D_Q), lambda b, t: (layer, 0, 0))

    yspec = pl.BlockSpec((BT_MIX, D_Q), lambda b, t: (b * nt + t, 0))
    yshape = jax.ShapeDtypeStruct((T_PROMPT, D_Q), BF16)
    return pl.pallas_call(
        _mix_acd_body,
        grid=(BATCH, nt),
        in_specs=[zcol(0), zcol(1), zcol(5), zcol(6), zcol(7), zcol(8), zcol(9),
                  vec(1), vec(1),
                  pl.BlockSpec((None, N_HEADS, CHUNK, CHUNK), lambda b, t: (layer, 0, 0, 0)),
                  pl.BlockSpec((None, CHUNK, N_HEADS), lambda b, t: (layer, 0, 0)),
                  vec(CONV_C), vec(1), vec(1), vec(1), vec(CONV_D)],
        out_specs=[yspec, yspec, yspec,
                   pl.BlockSpec((None, CONV_C - 1, D_Q), lambda b, t: (b, 0, 0)),
                   pl.BlockSpec((None, CONV_D - 1, D_Q), lambda b, t: (b, 0, 0))],
        out_shape=[yshape, yshape, yshape,
                   jax.ShapeDtypeStruct((BATCH, CONV_C - 1, D_Q), F32),
                   jax.ShapeDtypeStruct((BATCH, CONV_D - 1, D_Q), F32)],
        scratch_shapes=[pltpu.VMEM((BT_MIX, D_Q), BF16),
                        pltpu.VMEM((BT_MIX + 32, D_Q), F32),
                        pltpu.VMEM((BT_MIX + 8, D_Q), F32),
                        pltpu.VMEM((BT_MIX, D_Q), F32)],
        compiler_params=_params(("arbitrary", "arbitrary"), 40),
        name="mix_acd",
    )(z, z, z, z, z, z, z, a_ln_g, a_ln_b, a_ws, a_bs_t, c_conv_w, c_conv_b, c_ln_g, c_ln_b,
      d_conv_w)


def _attn_body(slopes_ref, q_ref, k_ref, v_ref, o_ref, qd, kd, vd, od, md, ld, ob, mb, lb):
    slope = slopes_ref[pl.program_id(1)]
    qi = lax.broadcasted_iota(jnp.int32, (128, 128), 0)
    ki = lax.broadcasted_iota(jnp.int32, (128, 128), 1)
    own_ok = ki <= qi
    dist_own = (qi - ki).astype(F32)
    dist_prev = (qi + 128 - ki).astype(F32)
    nblk = SEQ // 128

    for g, d in enumerate(DILATIONS):
        sub = SEQ // d
        for r in range(d):
            src = pl.ds(r, sub, stride=d) if d > 1 else slice(0, SEQ)
            dst = slice(r * sub, (r + 1) * sub)
            qd[dst, :] = (q_ref[src, :] * ATTN_SCALE).astype(BF16)
            kd[dst, :] = k_ref[src, :].astype(BF16)
            vd[dst, :] = v_ref[src, :].astype(BF16)
        per_sub = sub // 128
        pen = slope * float(d)

        def block(i, carry):
            off = pl.multiple_of(i * 128, 128)
            poff = pl.multiple_of(jnp.maximum(i - 1, 0) * 128, 128)
            no_prev = jnp.where((i % per_sub) == 0, 2 * 128, 0)
            q = qd[pl.ds(off, 128), :]
            dn = (((1,), (1,)), ((), ()))
            s1 = lax.dot_general(q, kd[pl.ds(off, 128), :], dn, preferred_element_type=F32)
            s0 = lax.dot_general(q, kd[pl.ds(poff, 128), :], dn, preferred_element_type=F32)
            s1 = jnp.where(own_ok, s1 - pen * dist_own, NEG)
            s0 = jnp.where(ki >= qi + no_prev, s0 - pen * dist_prev, NEG)
            m = jnp.maximum(jnp.max(s1, axis=-1, keepdims=True), jnp.max(s0, axis=-1, keepdims=True))
            p1 = jnp.exp(s1 - m)
            p0 = jnp.exp(s0 - m)
            den = jnp.sum(p1, axis=-1, keepdims=True) + jnp.sum(p0, axis=-1, keepdims=True)
            o = (jnp.dot(p1.astype(BF16), vd[pl.ds(off, 128), :], preferred_element_type=F32)
                 + jnp.dot(p0.astype(BF16), vd[pl.ds(poff, 128), :], preferred_element_type=F32))
            od[pl.ds(off, 128), :] = o
            md[pl.ds(off, 128), :] = jnp.broadcast_to(m, (128, 128))
            ld[pl.ds(off, 128), :] = jnp.broadcast_to(den, (128, 128))
            return carry

        lax.fori_loop(0, nblk, block, 0)
        for r in range(d):
            dst = pl.ds(r, sub, stride=d) if d > 1 else slice(0, SEQ)
            src = slice(r * sub, (r + 1) * sub)
            ob[g, dst, :] = od[src, :]
            mb[g, dst, :] = md[src, :]
            lb[g, dst, :] = ld[src, :]

    m_all = jnp.maximum(jnp.maximum(mb[0], mb[1]), mb[2])
    num = jnp.zeros((SEQ, HEAD_DIM), F32)
    den = jnp.zeros((SEQ, HEAD_DIM), F32)
    for g in range(len(DILATIONS)):
        e = jnp.exp(mb[g] - m_all)
        num = num + e * ob[g]
        den = den + e * lb[g]
    o_ref[...] = (num / den).astype(BF16)


def _attn_prompt(z, slopes):
    big = pltpu.VMEM((SEQ, HEAD_DIM), F32)
    small = pltpu.VMEM((SEQ, HEAD_DIM), BF16)
    tri = pltpu.VMEM((len(DILATIONS), SEQ, HEAD_DIM), F32)
    col0 = 2 * N_HEADS
    return pl.pallas_call(
        _attn_body,
        grid_spec=pltpu.PrefetchScalarGridSpec(
            num_scalar_prefetch=1,
            grid=(BATCH, N_HEADS),
            in_specs=[pl.BlockSpec((SEQ, HEAD_DIM), lambda b, h, s: (b, col0 + h)),
                      pl.BlockSpec((SEQ, HEAD_DIM), lambda b, h, s: (b, col0 + N_HEADS + h)),
                      pl.BlockSpec((SEQ, HEAD_DIM), lambda b, h, s: (b, col0 + 2 * N_HEADS + h))],
            out_specs=pl.BlockSpec((SEQ, HEAD_DIM), lambda b, h, s: (b, h)),
            scratch_shapes=[small, small, small, big, big, big, tri, tri, tri]),
        out_shape=jax.ShapeDtypeStruct((T_PROMPT, D_Q), BF16),
        compiler_params=_params(("arbitrary", "arbitrary"), 40),
        name="attn_prompt",
    )(slopes, z, z, z)


def _layer_norm_tile(y, g, b):
    mu = jnp.mean(y, axis=(0, 1), keepdims=True)
    yc = y - mu
    var = jnp.mean(yc * yc, axis=(0, 1), keepdims=True)
    return yc * lax.rsqrt(var + LN_EPS) * g + b


def _sample_body(z_ref, slope_ref, ck_ref, cv_ref, pc_ref, pd_ref,
                 lng_ref, lnb_ref, ws0_ref, bs0_ref, cw_ref, cb_ref, clg_ref, clb_ref, dw_ref,
                 mix_ref, sc_ref, sd_ref, va_ref):
    def zc(c):
        return z_ref[c * N_HEADS:(c + 1) * N_HEADS, :]

    u = jax.nn.gelu(zc(0))
    v = _layer_norm_tile(jax.nn.gelu(zc(1)), lng_ref[...], lnb_ref[...])
    va_ref[...] = v
    mix_ref[0:N_HEADS, :] = u * (ws0_ref[...] * v + bs0_ref[...])

    q = zc(2) * ATTN_SCALE
    k_new = zc(3)
    v_new = zc(4)
    s_new = jnp.sum(q * k_new, axis=-1, keepdims=True)
    jdist = (N_BACK - lax.broadcasted_iota(jnp.int32, (N_BACK, 1, 1), 0)).astype(F32)
    ms, ls, outs = [], [], []
    for d in DILATIONS:
        rows = pl.ds(WIN - N_BACK * d, N_BACK, stride=d) if d > 1 else pl.ds(WIN - N_BACK, N_BACK)
        s = jnp.sum(ck_ref[rows] * q[None], axis=-1, keepdims=True)
        s = s - (float(d) * slope_ref[...])[None] * jdist
        m = jnp.maximum(jnp.max(s, axis=0), s_new)
        p = jnp.exp(s - m[None])
        p_new = jnp.exp(s_new - m)
        ls.append(jnp.sum(p, axis=0) + p_new)
        outs.append(jnp.sum(p * cv_ref[rows], axis=0) + p_new * v_new)
        ms.append(m)
    m_all = jnp.maximum(jnp.maximum(ms[0], ms[1]), ms[2])
    num = jnp.zeros((N_HEADS, HEAD_DIM), F32)
    den = jnp.zeros((N_HEADS, 1), F32)
    for m, l, o in zip(ms, ls, outs):
        e = jnp.exp(m - m_all)
        num = num + e * o
        den = den + e * l
    mix_ref[N_HEADS:2 * N_HEADS, :] = num / den

    hc = zc(5) * _sigmoid(zc(6))
    y = jnp.sum(cw_ref[0:CONV_C - 1] * pc_ref[...], axis=0) + cw_ref[CONV_C - 1] * hc + cb_ref[...]
    y = _layer_norm_tile(y, clg_ref[...], clb_ref[...])
    mix_ref[2 * N_HEADS:3 * N_HEADS, :] = y * _sigmoid(y)
    sc_ref[0:CONV_C - 2] = pc_ref[1:CONV_C - 1]
    sc_ref[CONV_C - 2] = hc

    ch = zc(8) * zc(9)
    conv = dw_ref[0] * pd_ref[0] + dw_ref[1] * pd_ref[1] + dw_ref[2] * ch
    mix_ref[3 * N_HEADS:4 * N_HEADS, :] = zc(7) * conv
    sd_ref[0] = pd_ref[1]
    sd_ref[1] = ch


def _sample_mixers(slopes, z_s, cache_k, cache_v, state_c, state_d, layer,
                   a_ln_g, a_ln_b, ws0, bs0, c_conv_w, c_conv_b, c_ln_g, c_ln_b, d_conv_w):
    tile = (N_HEADS, HEAD_DIM)
    vec = pl.BlockSpec((None,) + tile, lambda b: (layer, 0, 0))

    def taps(rows):
        return pl.BlockSpec((None, rows) + tile, lambda b: (layer, 0, 0, 0))

    def cache_spec(rows):
        return pl.BlockSpec((None, None, rows) + tile, lambda b: (layer, b, 0, 0, 0))

    def per_seq(rows):
        return pl.BlockSpec((None, rows) + tile, lambda b: (b, 0, 0, 0))

    def flat(rows):
        return pl.BlockSpec((None, rows, HEAD_DIM), lambda b: (b, 0, 0))

    return pl.pallas_call(
        _sample_body,
        grid=(DEC_BATCH,),
        in_specs=[flat(D_IN // HEAD_DIM), pl.BlockSpec((N_HEADS, 1), lambda b: (0, 0)),
                  cache_spec(WIN), cache_spec(WIN), cache_spec(CONV_C - 1), cache_spec(CONV_D - 1),
                  vec, vec, vec, vec, taps(CONV_C), vec, vec, vec, taps(CONV_D)],
        out_specs=[flat(D_MODEL // HEAD_DIM), per_seq(CONV_C - 1), per_seq(CONV_D - 1), flat(N_HEADS)],
        out_shape=[jax.ShapeDtypeStruct((DEC_BATCH, D_MODEL // HEAD_DIM, HEAD_DIM), F32),
                   jax.ShapeDtypeStruct((DEC_BATCH, CONV_C - 1) + tile, F32),
                   jax.ShapeDtypeStruct((DEC_BATCH, CONV_D - 1) + tile, F32),
                   jax.ShapeDtypeStruct((DEC_BATCH,) + tile, F32)],
        compiler_params=_params(("arbitrary",), 48),
        name="sample_mixers",
    )(z_s, slopes, cache_k, cache_v, state_c, state_d, a_ln_g, a_ln_b, ws0, bs0,
      c_conv_w, c_conv_b, c_ln_g, c_ln_b, d_conv_w)


def _ln_body(x_ref, f_ref, g_ref, b_ref, of_ref, ob_ref):
    y = _layer_norm_rows(ALPHA * x_ref[...] + f_ref[...], g_ref[...], b_ref[...])
    of_ref[...] = y
    ob_ref[...] = y.astype(BF16)


def _route(y, rh_ref, rl_ref):
    yh = y.astype(BF16)
    yl = (y - yh.astype(F32)).astype(BF16)
    logits = (jnp.dot(yh, rh_ref[...], preferred_element_type=F32)
              + jnp.dot(yh, rl_ref[...], preferred_element_type=F32)
              + jnp.dot(yl, rh_ref[...], preferred_element_type=F32))
    lane = lax.broadcasted_iota(jnp.int32, logits.shape, 1)
    lane_f = lane.astype(F32)
    logits = jnp.where(lane < N_EXPERTS, logits, NEG)
    t1 = jnp.max(logits, axis=-1, keepdims=True)
    i1 = jnp.min(jnp.where(logits == t1, lane_f, 128.0), axis=-1, keepdims=True)
    rest = jnp.where(lane_f == i1, NEG, logits)
    t2 = jnp.max(rest, axis=-1, keepdims=True)
    i2 = jnp.min(jnp.where(rest == t2, lane_f, 128.0), axis=-1, keepdims=True)
    e2 = jnp.exp(t2 - t1)
    g1 = 1.0 / (1.0 + e2)
    g2 = e2 / (1.0 + e2)
    out = jnp.where(lane == 0, i1, 0.0)
    out = jnp.where(lane == 1, i2, out)
    out = jnp.where(lane == 2, g1, out)
    return jnp.where(lane == 3, g2, out)


def _ln_route_body(x_ref, f_ref, g_ref, b_ref, rh_ref, rl_ref, of_ref, ob_ref, rt_ref):
    y = _layer_norm_rows(ALPHA * x_ref[...] + f_ref[...], g_ref[...], b_ref[...])
    of_ref[...] = y
    ob_ref[...] = y.astype(BF16)
    rt_ref[...] = _route(y, rh_ref, rl_ref)


def _ln(x, f, g, b, layer, router=None):
    row = pl.BlockSpec((BM_LN, D_MODEL), lambda i: (i, 0))
    vec = pl.BlockSpec((None, 1, D_MODEL), lambda i: (layer, 0, 0))
    in_specs = [row, row, vec, vec]
    out_specs = [row, row]
    out_shape = [jax.ShapeDtypeStruct((T_PAD, D_MODEL), F32), jax.ShapeDtypeStruct((T_PAD, D_MODEL), BF16)]
    args = [x, f, g, b]
    body = _ln_body
    if router is not None:
        rspec = pl.BlockSpec((D_MODEL, 128), lambda i: (0, 0))
        in_specs += [rspec, rspec]
        out_specs.append(pl.BlockSpec((BM_LN, 128), lambda i: (i, 0)))
        out_shape.append(jax.ShapeDtypeStruct((T_PAD, 128), F32))
        args += list(router)
        body = _ln_route_body
    return pl.pallas_call(
        body, grid=(T_PAD // BM_LN,), in_specs=in_specs, out_specs=out_specs, out_shape=out_shape,
        compiler_params=_params(("arbitrary",), 40), name="residual_ln",
    )(*args)


def _ff_up_body(te_ref, tr_ref, x_ref, w1_ref, w3_ref, h_ref, w1b, w3b, *, bm, sb):
    rows = tr_ref[pl.program_id(0)]

    @pl.when(rows > 0)
    def _():
        _cast_rows(w1b, w1_ref, w1b.shape[0])
        _cast_rows(w3b, w3_ref, w3b.shape[0])

    for s in range(bm // sb):
        rs = slice(s * sb, (s + 1) * sb)

        @pl.when(s * sb < rows)
        def _():
            xs = x_ref[rs, :]
            a = jnp.dot(xs, w1b[...], preferred_element_type=F32)
            g = jnp.dot(xs, w3b[...], preferred_element_type=F32)
            h_ref[rs, :] = (a * _sigmoid(a) * g).astype(BF16)

        @pl.when(s * sb >= rows)
        def _():
            h_ref[rs, :] = jnp.zeros((sb, h_ref.shape[1]), BF16)


def _ff_up(x_bf, w1, w3, layer, tile_expert, tile_rows, bm, sb):
    s_rows = x_bf.shape[0]
    f = w1.shape[-1]
    nj = f // FF_BN

    def wmap(i, j, te, tr):
        used = (tr[i] > 0).astype(jnp.int32)
        return (layer, te[i], 0, j * used + (nj - 1) * (1 - used))

    wspec = pl.BlockSpec((None, None, D_MODEL, FF_BN), wmap)
    return pl.pallas_call(
        functools.partial(_ff_up_body, bm=bm, sb=sb),
        grid_spec=pltpu.PrefetchScalarGridSpec(
            num_scalar_prefetch=2,
            grid=(s_rows // bm, nj),
            in_specs=[pl.BlockSpec((bm, D_MODEL), lambda i, j, te, tr: (i, 0)), wspec, wspec],
            out_specs=pl.BlockSpec((bm, FF_BN), lambda i, j, te, tr: (i, j)),
            scratch_shapes=[pltpu.VMEM((D_MODEL, FF_BN), BF16), pltpu.VMEM((D_MODEL, FF_BN), BF16)]),
        out_shape=jax.ShapeDtypeStruct((s_rows, f), BF16),
        compiler_params=_params(("arbitrary", "arbitrary"), 52),
        name="ff_up",
    )(tile_expert, tile_rows, x_bf, w1, w3)


def _ff_down_body(te_ref, tr_ref, h_ref, w2_ref, o_ref, wb, *, bm, sb, f):
    rows = tr_ref[pl.program_id(0)]
    k = pl.program_id(2)
    valid = f - k * FF_TK

    @pl.when(k == 0)
    def _():
        o_ref[...] = jnp.zeros(o_ref.shape, F32)

    @pl.when(rows > 0)
    def _():
        if f % FF_TK:
            rid = lax.broadcasted_iota(jnp.int32, w2_ref.shape, 0)
            wb[...] = jnp.where(rid < valid, w2_ref[...], 0.0).astype(BF16)
        else:
            _cast_rows(wb, w2_ref, FF_TK)

    for s in range(bm // sb):
        rs = slice(s * sb, (s + 1) * sb)

        @pl.when(s * sb < rows)
        def _():
            hs = h_ref[rs, :]
            if f % FF_TK:
                cid = lax.broadcasted_iota(jnp.int32, hs.shape, 1)
                hs = jnp.where(cid < valid, hs, jnp.zeros_like(hs))
            o_ref[rs, :] += jnp.dot(hs, wb[...], preferred_element_type=F32)


def _ff_down(h, w2, layer, tile_expert, tile_rows, bm, sb):
    s_rows, f = h.shape
    nk = pl.cdiv(f, FF_TK)

    def wmap(i, j, k, te, tr):
        used = (tr[i] > 0).astype(jnp.int32)
        return (layer, te[i], k * used + (nk - 1) * (1 - used), j * used + (1 - used))

    return pl.pallas_call(
        functools.partial(_ff_down_body, bm=bm, sb=sb, f=f),
        grid_spec=pltpu.PrefetchScalarGridSpec(
            num_scalar_prefetch=2,
            grid=(s_rows // bm, D_MODEL // FF_BN_DOWN, nk),
            in_specs=[pl.BlockSpec((bm, FF_TK), lambda i, j, k, te, tr: (i, k)),
                      pl.BlockSpec((None, None, FF_TK, FF_BN_DOWN), wmap)],
            out_specs=pl.BlockSpec((bm, FF_BN_DOWN), lambda i, j, k, te, tr: (i, j)),
            scratch_shapes=[pltpu.VMEM((FF_TK, FF_BN_DOWN), BF16)]),
        out_shape=jax.ShapeDtypeStruct((s_rows, D_MODEL), F32),
        compiler_params=_params(("arbitrary", "arbitrary", "arbitrary"), 52),
        name="ff_down",
    )(tile_expert, tile_rows, h, w2)


def _row_copy(src_hbm, row, buf, r, sem):
    return pltpu.make_async_copy(src_hbm.at[pl.ds(row, 1), :], buf.at[pl.ds(r, 1), :], sem)


def _gather_body(src_ref, tr_ref, x_hbm, o_ref, buf, sem):
    base = pl.program_id(0) * R_GATHER
    tile = base // BM_MOE
    nvalid = jnp.clip(tr_ref[tile] - (base - tile * BM_MOE), 0, R_GATHER)

    def start(r, c):
        @pl.when(r < nvalid)
        def _():
            _row_copy(x_hbm, src_ref[base + r], buf, r, sem).start()
        return c

    def wait(r, c):
        @pl.when(r < nvalid)
        def _():
            _row_copy(x_hbm, 0, buf, r, sem).wait()
        return c

    lax.fori_loop(0, R_GATHER, start, 0)
    lax.fori_loop(0, R_GATHER, wait, 0)
    rid = lax.broadcasted_iota(jnp.int32, (R_GATHER, 1), 0)
    o_ref[...] = jnp.where(rid < nvalid, buf[...], 0.0).astype(BF16)


def _gather_rows(x_f32, src_rows, tile_rows):
    return pl.pallas_call(
        _gather_body,
        grid_spec=pltpu.PrefetchScalarGridSpec(
            num_scalar_prefetch=2,
            grid=(S_PAD // R_GATHER,),
            in_specs=[pl.BlockSpec(memory_space=pl.ANY)],
            out_specs=pl.BlockSpec((R_GATHER, D_MODEL), lambda i, s, t: (i, 0)),
            scratch_shapes=[pltpu.VMEM((R_GATHER, D_MODEL), F32), pltpu.SemaphoreType.DMA(())]),
        out_shape=jax.ShapeDtypeStruct((S_PAD, D_MODEL), BF16),
        compiler_params=_params(("arbitrary",), 32),
        name="moe_gather",
    )(src_rows, tile_rows, x_f32)


def _combine_body(pos_ref, x_ref, gate_ref, g_ref, b_ref, y_hbm, of_ref, ob_ref, buf0, buf1, sem):
    base = pl.program_id(0) * R_COMBINE
    nvalid = jnp.clip(T_REAL - base, 0, R_COMBINE)

    def start(r, c):
        @pl.when(r < nvalid)
        def _():
            _row_copy(y_hbm, pos_ref[2 * (base + r)], buf0, r, sem).start()
            _row_copy(y_hbm, pos_ref[2 * (base + r) + 1], buf1, r, sem).start()
        return c

    def wait(r, c):
        @pl.when(r < nvalid)
        def _():
            _row_copy(y_hbm, 0, buf0, r, sem).wait()
            _row_copy(y_hbm, 0, buf1, r, sem).wait()
        return c

    lax.fori_loop(0, R_COMBINE, start, 0)
    lax.fori_loop(0, R_COMBINE, wait, 0)
    rid = lax.broadcasted_iota(jnp.int32, (R_COMBINE, 1), 0)
    ffn = jnp.where(rid < nvalid, gate_ref[:, 0:1] * buf0[...] + gate_ref[:, 1:2] * buf1[...], 0.0)
    y = _layer_norm_rows(ALPHA * x_ref[...] + ffn, g_ref[...], b_ref[...])
    of_ref[...] = y
    ob_ref[...] = y.astype(BF16)


def _combine_ln(x, gates, g, b, layer, y_slots, pos):
    row = pl.BlockSpec((R_COMBINE, D_MODEL), lambda i, p: (i, 0))
    vec = pl.BlockSpec((None, 1, D_MODEL), lambda i, p: (layer, 0, 0))
    return pl.pallas_call(
        _combine_body,
        grid_spec=pltpu.PrefetchScalarGridSpec(
            num_scalar_prefetch=1,
            grid=(T_PAD // R_COMBINE,),
            in_specs=[row, pl.BlockSpec((R_COMBINE, TOP_K), lambda i, p: (i, 0)), vec, vec,
                      pl.BlockSpec(memory_space=pl.ANY)],
            out_specs=[row, row],
            scratch_shapes=[pltpu.VMEM((R_COMBINE, D_MODEL), F32), pltpu.VMEM((R_COMBINE, D_MODEL), F32),
                            pltpu.SemaphoreType.DMA(())]),
        out_shape=[jax.ShapeDtypeStruct((T_PAD, D_MODEL), F32), jax.ShapeDtypeStruct((T_PAD, D_MODEL), BF16)],
        compiler_params=_params(("arbitrary",), 32),
        name="moe_combine_ln",
    )(pos, x, gates, g, b, y_slots)


def _dispatch_plan(route):
    idx = route[:T_REAL, 0:TOP_K].astype(jnp.int32)
    flat = idx.reshape(-1)
    onehot = (flat[:, None] == jnp.arange(N_EXPERTS)[None, :]).astype(jnp.int32)
    counts = jnp.sum(onehot, axis=0)
    rank = jnp.sum((jnp.cumsum(onehot, axis=0) - 1) * onehot, axis=1)
    tiles_e = (counts + BM_MOE - 1) // BM_MOE
    tile_end = jnp.cumsum(tiles_e)
    tile_start = tile_end - tiles_e
    pos = tile_start[flat] * BM_MOE + rank
    tile_id = jnp.arange(MOE_TILES)
    tile_expert = jnp.minimum(jnp.sum((tile_id[:, None] >= tile_end[None, :]).astype(jnp.int32), axis=1),
                              N_EXPERTS - 1)
    last_used = jnp.max(jnp.where(counts > 0, jnp.arange(N_EXPERTS), 0))
    used = tile_id < tile_end[-1]
    tile_expert = jnp.where(used, tile_expert, last_used).astype(jnp.int32)
    tile_rows = jnp.clip(counts[tile_expert] - (tile_id - tile_start[tile_expert]) * BM_MOE, 0, BM_MOE)
    tile_rows = jnp.where(used, tile_rows, 0).astype(jnp.int32)
    src_rows = jnp.zeros((S_PAD,), jnp.int32).at[pos].set(jnp.arange(N_SLOTS, dtype=jnp.int32) // TOP_K)
    gates = jnp.pad(route[:T_REAL, 2:2 + TOP_K], ((0, T_PAD - T_REAL), (0, 0)))
    return tile_expert, tile_rows, src_rows, pos.astype(jnp.int32), gates


def kernel(x_prompt, x_sample, cache_b_k, cache_b_v, state_c, state_d, w_in, a_ln_g, a_ln_b, a_ws, a_bs, c_conv_w, c_conv_b, c_ln_g, c_ln_b, d_conv_w, w_out, ln1_g, ln1_b, ln2_g, ln2_b, ffn_w1, ffn_w3, ffn_w2, moe_router, moe_w1, moe_w3, moe_w2):
    slopes = jnp.asarray([2.0 ** (-8.0 * (h + 1) / N_HEADS) for h in range(N_HEADS)], F32)
    x = jnp.concatenate([x_prompt.reshape(T_PROMPT, D_MODEL), x_sample.reshape(DEC_BATCH, D_MODEL),
                         jnp.zeros((T_PAD - T_REAL, D_MODEL), F32)], axis=0)
    x_bf = x.astype(BF16)

    def vec3(p):
        return p.reshape(DEPTH, 1, -1)

    a_ln_g3, a_ln_b3, c_conv_b3, c_ln_g3, c_ln_b3 = map(vec3, (a_ln_g, a_ln_b, c_conv_b, c_ln_g, c_ln_b))
    ln1_g3, ln1_b3, ln2_g3, ln2_b3 = map(vec3, (ln1_g, ln1_b, ln2_g, ln2_b))
    a_bs_t = jnp.swapaxes(a_bs, 1, 2)

    def tiles(p):
        return p.reshape(p.shape[:-1] + (N_HEADS, HEAD_DIM))

    tile_shape = (DEPTH, N_HEADS, HEAD_DIM)
    ws0 = jnp.broadcast_to(a_ws[:, :, 0, 0][:, :, None], tile_shape)
    bs0 = jnp.broadcast_to(a_bs[:, :, 0][:, :, None], tile_shape)
    sample_params = (tiles(a_ln_g), tiles(a_ln_b), ws0, bs0, tiles(c_conv_w), tiles(c_conv_b),
                     tiles(c_ln_g), tiles(c_ln_b), tiles(d_conv_w))
    state_c_t, state_d_t = tiles(state_c), tiles(state_d)
    slope_col = slopes.reshape(N_HEADS, 1)
    dense_tiles = T_PAD // BM
    dense_expert = jnp.zeros((dense_tiles,), jnp.int32)
    dense_rows = jnp.full((dense_tiles,), BM, jnp.int32)

    pk, pv, sk, sv, pc, scs, pd, sds, sa = [], [], [], [], [], [], [], [], []
    for l in range(DEPTH):
        z = _proj(x_bf, w_in, l)
        ya, yc, yd, c_buf, d_buf = _mix_acd(z, l, a_ln_g3, a_ln_b3, a_ws, a_bs_t, c_conv_w, c_conv_b3,
                                            c_ln_g3, c_ln_b3, d_conv_w)
        yb = _attn_prompt(z, slopes)
        z_s = z[T_PROMPT:T_REAL].reshape(DEC_BATCH, D_IN // HEAD_DIM, HEAD_DIM)
        mix_s, c_buf_s, d_buf_s, v_a_s = _sample_mixers(
            slope_col, z_s, cache_b_k, cache_b_v, state_c_t, state_d_t, l, *sample_params)
        mixcat = jnp.concatenate(
            [jnp.concatenate([ya, yb, yc, yd], axis=1),
             mix_s.reshape(DEC_BATCH, D_MODEL).astype(BF16),
             jnp.zeros((T_PAD - T_REAL, D_MODEL), BF16)], axis=0)
        mix = _proj(mixcat, w_out, l)

        zp = z[:T_PROMPT].reshape(BATCH, SEQ, D_IN)
        pk.append(zp[:, :, 3 * D_Q:4 * D_Q].reshape(BATCH, SEQ, N_HEADS, HEAD_DIM))
        pv.append(zp[:, :, 4 * D_Q:5 * D_Q].reshape(BATCH, SEQ, N_HEADS, HEAD_DIM))
        sk.append(z_s[:, 3 * N_HEADS:4 * N_HEADS].reshape(DEC_BATCH, 1, N_HEADS, HEAD_DIM))
        sv.append(z_s[:, 4 * N_HEADS:5 * N_HEADS].reshape(DEC_BATCH, 1, N_HEADS, HEAD_DIM))
        pc.append(c_buf)
        scs.append(c_buf_s.reshape(DEC_BATCH, CONV_C - 1, D_Q))
        pd.append(d_buf)
        sds.append(d_buf_s.reshape(DEC_BATCH, CONV_D - 1, D_Q))
        sa.append(v_a_s.reshape(DEC_BATCH, 1, D_Q))

        i = l // 2
        if l % 2 == 0:
            x1, x1_bf = _ln(x, mix, ln1_g3, ln1_b3, l)
            hid = _ff_up(x1_bf, ffn_w1[:, None], ffn_w3[:, None], i, dense_expert, dense_rows, BM, BM)
            ffn = _ff_down(hid, ffn_w2[:, None], i, dense_expert, dense_rows, BM, BM)
            x, x_bf = _ln(x1, ffn, ln2_g3, ln2_b3, l)
        else:
            r = jnp.pad(moe_router[i], ((0, 0), (0, 128 - N_EXPERTS)))
            r_hi = r.astype(BF16)
            r_lo = (r - r_hi.astype(F32)).astype(BF16)
            x1, x1_bf, route = _ln(x, mix, ln1_g3, ln1_b3, l, router=(r_hi, r_lo))
            tile_expert, tile_rows, src_rows, pos, gates = _dispatch_plan(route)
            xs = _gather_rows(x1, src_rows, tile_rows)
            hid = _ff_up(xs, moe_w1, moe_w3, i, tile_expert, tile_rows, BM_MOE, SB_MOE)
            y_slots = _ff_down(hid, moe_w2, i, tile_expert, tile_rows, BM_MOE, SB_MOE)
            x, x_bf = _combine_ln(x1, gates, ln2_g3, ln2_b3, l, y_slots, pos)

    return (x[:T_PROMPT].reshape(BATCH, SEQ, D_MODEL), x[T_PROMPT:T_REAL].reshape(DEC_BATCH, 1, D_MODEL),
            jnp.stack(pk), jnp.stack(pv), jnp.stack(sk), jnp.stack(sv),
            jnp.stack(pc), jnp.stack(scs), jnp.stack(pd), jnp.stack(sds), jnp.stack(sa))
```

```python
import functools

import jax
import jax.numpy as jnp
from jax import lax
from jax.experimental import pallas as pl
from jax.experimental.pallas import tpu as pltpu

D_MODEL = 4096
BATCH = 4
SEQ = 2048
DEPTH = 2
DEC_BATCH = 8
HEAD_DIM = 128
D_Q = D_MODEL // 4
N_HEADS = D_Q // HEAD_DIM
D_IN = 10 * D_Q
CHUNK = 128
DILATIONS = (1, 4, 16)
N_BACK = 128
WIN = 2048
CONV_C = 31
CONV_D = 3
D_FF = 11008
N_EXPERTS = 8
TOP_K = 2
D_FF_EXPERT = 14336
ALPHA = (2.0 * DEPTH) ** 0.25
LN_EPS = 1e-5
ATTN_SCALE = HEAD_DIM ** -0.5
NEG = -1e30

T_PROMPT = BATCH * SEQ
T_REAL = T_PROMPT + DEC_BATCH
T_PAD = 8448
BM = 1056
BN_PROJ = 512
BT_MIX = 256
BM_LN = 256
FF_BN = 256
FF_TK = 1024
FF_BN_DOWN = 2048
BM_MOE = 1280
SB_MOE = 128
N_SLOTS = T_REAL * TOP_K
MOE_TILES = (N_SLOTS + N_EXPERTS * (BM_MOE - 1)) // BM_MOE
S_PAD = MOE_TILES * BM_MOE
R_GATHER = 256
R_COMBINE = 128
V7X_VMEM_BYTES = 64 * 1024 * 1024

F32 = jnp.float32
BF16 = jnp.bfloat16


def _params(sem, vmem_mb):
    assert vmem_mb * 1024 * 1024 < V7X_VMEM_BYTES
    return pltpu.CompilerParams(dimension_semantics=sem, vmem_limit_bytes=vmem_mb * 1024 * 1024)


def _sigmoid(x):
    return 1.0 / (1.0 + jnp.exp(-x))


def _layer_norm_rows(y, g, b):
    mu = jnp.mean(y, axis=-1, keepdims=True)
    yc = y - mu
    var = jnp.mean(yc * yc, axis=-1, keepdims=True)
    return yc * lax.rsqrt(var + LN_EPS) * g + b


def _cast_rows(dst_ref, src_ref, rows, chunk=256):
    def body(i, c):
        sl = pl.ds(pl.multiple_of(i * chunk, chunk), chunk)
        dst_ref[sl, :] = src_ref[sl, :].astype(BF16)
        return c
    lax.fori_loop(0, rows // chunk, body, 0)


def _proj_body(x_ref, w_ref, o_ref, wb_ref):
    @pl.when(pl.program_id(1) == 0)
    def _():
        _cast_rows(wb_ref, w_ref, wb_ref.shape[0])
    o_ref[...] = jnp.dot(x_ref[...], wb_ref[...], preferred_element_type=F32)


def _proj(x_bf, w_stack, layer):
    m, k = x_bf.shape
    n = w_stack.shape[-1]
    return pl.pallas_call(
        _proj_body,
        grid=(n // BN_PROJ, m // BM),
        in_specs=[pl.BlockSpec((BM, k), lambda j, i: (i, 0)),
                  pl.BlockSpec((None, k, BN_PROJ), lambda j, i: (layer, 0, j))],
        out_specs=pl.BlockSpec((BM, BN_PROJ), lambda j, i: (i, j)),
        out_shape=jax.ShapeDtypeStruct((m, n), F32),
        scratch_shapes=[pltpu.VMEM((k, BN_PROJ), BF16)],
        compiler_params=_params(("arbitrary", "arbitrary"), 48),
        name="proj",
    )(x_bf, w_stack)


def _mix_acd_body(up_ref, vp_ref, zk_ref, zv_ref, ca_ref, cg_ref, dbg_ref, dcg_ref, dh_ref, yb_ref, mixs_ref,
                  lng_ref, lnb_ref, ws_ref, bst_ref, cw_ref, cb_ref, clg_ref, clb_ref, dw_ref,
                  pk_in_ref, pv_in_ref,
                  mix_ref, sc_ref, sd_ref, pk_ref, pv_ref, vbuf, hext, chext, ybuf):
    del pk_in_ref, pv_in_ref
    s = pl.program_id(0)
    nt = SEQ // BT_MIX
    t = s % nt
    bt = BT_MIX

    @pl.when(s == BATCH * nt)
    def _():
        pad = jnp.zeros((bt - DEC_BATCH, D_MODEL), F32)
        mix_ref[...] = jnp.concatenate([mixs_ref[...], pad], axis=0).astype(BF16)

    @pl.when(s < BATCH * nt)
    def _():
        @pl.when(t == 0)
        def _():
            hext[0:32, :] = jnp.zeros((32, D_Q), F32)
            chext[0:8, :] = jnp.zeros((8, D_Q), F32)

        pk_ref[...] = zk_ref[...]
        pv_ref[...] = zv_ref[...]
        mix_ref[:, D_Q:2 * D_Q] = yb_ref[...]

        v = jax.nn.gelu(vp_ref[...])
        vbuf[...] = _layer_norm_rows(v, lng_ref[...], lnb_ref[...]).astype(BF16)
        row = lax.broadcasted_iota(jnp.int32, (CHUNK, CHUNK), 0)
        col = lax.broadcasted_iota(jnp.int32, (CHUNK, CHUNK), 1)
        for h in range(N_HEADS):
            cs = slice(h * HEAD_DIM, (h + 1) * HEAD_DIM)
            w = jnp.where(row >= col, ws_ref[h], 0.0).astype(BF16)
            bias = bst_ref[:, h:h + 1]
            for c in range(bt // CHUNK):
                rs = slice(c * CHUNK, (c + 1) * CHUNK)
                mixed = jnp.dot(w, vbuf[rs, cs], preferred_element_type=F32) + bias
                mix_ref[rs, cs] = (jax.nn.gelu(up_ref[rs, cs]) * mixed).astype(BF16)

        hext[32:32 + bt, :] = ca_ref[...] * _sigmoid(cg_ref[...])
        rc = 64
        for cc in range(D_Q // 128):
            cs = slice(cc * 128, (cc + 1) * 128)
            for r0 in range(0, bt, rc):
                acc = jnp.zeros((rc, 128), F32) + cb_ref[:, cs]
                for k in range(CONV_C):
                    acc = acc + cw_ref[k:k + 1, cs] * hext[r0 + 2 + k:r0 + 2 + k + rc, cs]
                ybuf[r0:r0 + rc, cs] = acc
        y = _layer_norm_rows(ybuf[...], clg_ref[...], clb_ref[...])
        mix_ref[:, 2 * D_Q:3 * D_Q] = (y * _sigmoid(y)).astype(BF16)

        @pl.when(t == nt - 1)
        def _():
            sc_ref[...] = hext[bt + 2:bt + 32, :]
        hext[0:32, :] = hext[bt:bt + 32, :]

        chext[8:8 + bt, :] = dcg_ref[...] * dh_ref[...]
        conv = (dw_ref[0:1, :] * chext[6:6 + bt, :] + dw_ref[1:2, :] * chext[7:7 + bt, :]
                + dw_ref[2:3, :] * chext[8:8 + bt, :])
        mix_ref[:, 3 * D_Q:4 * D_Q] = (dbg_ref[...] * conv).astype(BF16)

        @pl.when(t == nt - 1)
        def _():
            sd_ref[...] = chext[bt + 6:bt + 8, :]
        chext[0:8, :] = chext[bt:bt + 8, :]


def _mix_acd(z, yb, mix_s, layer, kv_prev, a_ln_g, a_ln_b, a_ws, a_bs_t, c_conv_w, c_conv_b, c_ln_g,
             c_ln_b, d_conv_w):
    nt = SEQ // BT_MIX
    last = BATCH * nt - 1

    def zcol(c):
        return pl.BlockSpec((BT_MIX, D_Q), lambda s: (s, c))

    def vec(rows):
        return pl.BlockSpec((None, rows, D_Q), lambda s: (layer, 0, 0))

    def state(rows):
        return pl.BlockSpec((None, rows, D_Q), lambda s: (jnp.minimum(s // nt, BATCH - 1), 0, 0))

    kv_spec = pl.BlockSpec((None, BT_MIX, D_Q), lambda s: (layer, jnp.minimum(s, last), 0))
    kv_shape = jax.ShapeDtypeStruct((DEPTH, T_PROMPT, D_Q), F32)
    in_specs = [zcol(0), zcol(1), zcol(3), zcol(4), zcol(5), zcol(6), zcol(7), zcol(8), zcol(9),
                pl.BlockSpec((BT_MIX, D_Q), lambda s: (jnp.minimum(s, last), 0)),
                pl.BlockSpec((DEC_BATCH, D_MODEL), lambda s: (0, 0)),
                vec(1), vec(1),
                pl.BlockSpec((None, N_HEADS, CHUNK, CHUNK), lambda s: (layer, 0, 0, 0)),
                pl.BlockSpec((None, CHUNK, N_HEADS), lambda s: (layer, 0, 0)),
                vec(CONV_C), vec(1), vec(1), vec(1), vec(CONV_D),
                pl.BlockSpec(memory_space=pl.ANY), pl.BlockSpec(memory_space=pl.ANY)]
    args = [z] * 9 + [yb, mix_s, a_ln_g, a_ln_b, a_ws, a_bs_t, c_conv_w, c_conv_b, c_ln_g, c_ln_b, d_conv_w,
                      *kv_prev]
    aliases = {len(args) - 2: 3, len(args) - 1: 4}
    return pl.pallas_call(
        _mix_acd_body,
        grid=(T_PAD // BT_MIX,),
        in_specs=in_specs,
        out_specs=[pl.BlockSpec((BT_MIX, D_MODEL), lambda s: (s, 0)), state(CONV_C - 1), state(CONV_D - 1),
                   kv_spec, kv_spec],
        out_shape=[jax.ShapeDtypeStruct((T_PAD, D_MODEL), BF16),
                   jax.ShapeDtypeStruct((BATCH, CONV_C - 1, D_Q), F32),
                   jax.ShapeDtypeStruct((BATCH, CONV_D - 1, D_Q), F32), kv_shape, kv_shape],
        scratch_shapes=[pltpu.VMEM((BT_MIX, D_Q), BF16),
                        pltpu.VMEM((BT_MIX + 32, D_Q), F32),
                        pltpu.VMEM((BT_MIX + 8, D_Q), F32),
                        pltpu.VMEM((BT_MIX, D_Q), F32)],
        input_output_aliases=aliases,
        compiler_params=_params(("arbitrary",), 48),
        name="mix_acd",
    )(*args)


def _attn_body(slopes_ref, q_ref, k_ref, v_ref, o_ref, qd, kd, vd, od, md, ld, ob, mb, lb):
    slope = slopes_ref[pl.program_id(1)]
    qi = lax.broadcasted_iota(jnp.int32, (128, 256), 0)
    ci = lax.broadcasted_iota(jnp.int32, (128, 256), 1)
    dist = qi + 128 - ci
    in_band = jnp.logical_and(dist >= 0, dist <= N_BACK)
    nblk = SEQ // 128
    kd[0:128, :] = jnp.zeros((128, HEAD_DIM), BF16)
    vd[0:128, 0:HEAD_DIM] = jnp.zeros((128, HEAD_DIM), BF16)
    vd[:, HEAD_DIM:2 * HEAD_DIM] = jnp.ones((SEQ + 128, HEAD_DIM), BF16)

    for g, d in enumerate(DILATIONS):
        sub = SEQ // d
        for r in range(d):
            src = pl.ds(r, sub, stride=d) if d > 1 else slice(0, SEQ)
            qd[r * sub:(r + 1) * sub, :] = (q_ref[src, :] * ATTN_SCALE).astype(BF16)
            kd[128 + r * sub:128 + (r + 1) * sub, :] = k_ref[src, :].astype(BF16)
            vd[128 + r * sub:128 + (r + 1) * sub, 0:HEAD_DIM] = v_ref[src, :].astype(BF16)
        per_sub = sub // 128
        bias = jnp.where(in_band, (-slope * float(d)) * dist.astype(F32), NEG)

        def block(i, carry):
            off = pl.multiple_of(i * 128, 128)
            first_col = jnp.where((i % per_sub) == 0, 128, 0)
            s = lax.dot_general(qd[pl.ds(off, 128), :], kd[pl.ds(off, 256), :], (((1,), (1,)), ((), ())),
                                preferred_element_type=F32)
            s = jnp.where(ci >= first_col, s + bias, NEG)
            m = jnp.max(s, axis=-1, keepdims=True)
            p = jnp.exp(s - m).astype(BF16)
            acc = jnp.dot(p, vd[pl.ds(off, 256), :], preferred_element_type=F32)
            od[pl.ds(off, 128), :] = acc[:, 0:HEAD_DIM]
            ld[pl.ds(off, 128), :] = acc[:, HEAD_DIM:2 * HEAD_DIM]
            md[pl.ds(off, 128), :] = jnp.broadcast_to(m, (128, HEAD_DIM))
            return carry

        lax.fori_loop(0, nblk, block, 0, unroll=4)
        for r in range(d):
            dst = pl.ds(r, sub, stride=d) if d > 1 else slice(0, SEQ)
            src = slice(r * sub, (r + 1) * sub)
            ob[g, dst, :] = od[src, :]
            mb[g, dst, :] = md[src, :]
            lb[g, dst, :] = ld[src, :]

    m_all = jnp.maximum(jnp.maximum(mb[0], mb[1]), mb[2])
    num = jnp.zeros((SEQ, HEAD_DIM), F32)
    den = jnp.zeros((SEQ, HEAD_DIM), F32)
    for g in range(len(DILATIONS)):
        e = jnp.exp(mb[g] - m_all)
        num = num + e * ob[g]
        den = den + e * lb[g]
    o_ref[...] = (num / den).astype(BF16)


def _attn_prompt(z, slopes):
    big = pltpu.VMEM((SEQ, HEAD_DIM), F32)
    small = pltpu.VMEM((SEQ, HEAD_DIM), BF16)
    tri = pltpu.VMEM((len(DILATIONS), SEQ, HEAD_DIM), F32)
    col0 = 2 * N_HEADS
    return pl.pallas_call(
        _attn_body,
        grid_spec=pltpu.PrefetchScalarGridSpec(
            num_scalar_prefetch=1,
            grid=(BATCH, N_HEADS),
            in_specs=[pl.BlockSpec((SEQ, HEAD_DIM), lambda b, h, s: (b, col0 + h)),
                      pl.BlockSpec((SEQ, HEAD_DIM), lambda b, h, s: (b, col0 + N_HEADS + h)),
                      pl.BlockSpec((SEQ, HEAD_DIM), lambda b, h, s: (b, col0 + 2 * N_HEADS + h))],
            out_specs=pl.BlockSpec((SEQ, HEAD_DIM), lambda b, h, s: (b, h)),
            scratch_shapes=[small, pltpu.VMEM((SEQ + 128, HEAD_DIM), BF16),
                            pltpu.VMEM((SEQ + 128, 2 * HEAD_DIM), BF16), big, big, big, tri, tri, tri]),
        out_shape=jax.ShapeDtypeStruct((T_PROMPT, D_Q), BF16),
        compiler_params=_params(("arbitrary", "arbitrary"), 40),
        name="attn_prompt",
    )(slopes, z, z, z)


def _layer_norm_tile(y, g, b):
    mu = jnp.mean(y, axis=(0, 1), keepdims=True)
    yc = y - mu
    var = jnp.mean(yc * yc, axis=(0, 1), keepdims=True)
    return yc * lax.rsqrt(var + LN_EPS) * g + b


def _sample_body(z_ref, slope_ref, ck_ref, cv_ref, pc_ref, pd_ref,
                 lng_ref, lnb_ref, ws0_ref, bs0_ref, cw_ref, cb_ref, clg_ref, clb_ref, dw_ref,
                 mix_ref, sc_ref, sd_ref, va_ref):
    def zc(c):
        return z_ref[c * N_HEADS:(c + 1) * N_HEADS, :]

    u = jax.nn.gelu(zc(0))
    v = _layer_norm_tile(jax.nn.gelu(zc(1)), lng_ref[...], lnb_ref[...])
    va_ref[...] = v
    mix_ref[0:N_HEADS, :] = u * (ws0_ref[...] * v + bs0_ref[...])

    q = zc(2) * ATTN_SCALE
    k_new = zc(3)
    v_new = zc(4)
    s_new = jnp.sum(q * k_new, axis=-1, keepdims=True)
    jdist = (N_BACK - lax.broadcasted_iota(jnp.int32, (N_BACK, 1, 1), 0)).astype(F32)
    ms, ls, outs = [], [], []
    for d in DILATIONS:
        rows = pl.ds(WIN - N_BACK * d, N_BACK, stride=d) if d > 1 else pl.ds(WIN - N_BACK, N_BACK)
        s = jnp.sum(ck_ref[rows] * q[None], axis=-1, keepdims=True)
        s = s - (float(d) * slope_ref[...])[None] * jdist
        m = jnp.maximum(jnp.max(s, axis=0), s_new)
        p = jnp.exp(s - m[None])
        p_new = jnp.exp(s_new - m)
        ls.append(jnp.sum(p, axis=0) + p_new)
        outs.append(jnp.sum(p * cv_ref[rows], axis=0) + p_new * v_new)
        ms.append(m)
    m_all = jnp.maximum(jnp.maximum(ms[0], ms[1]), ms[2])
    num = jnp.zeros((N_HEADS, HEAD_DIM), F32)
    den = jnp.zeros((N_HEADS, 1), F32)
    for m, l, o in zip(ms, ls, outs):
        e = jnp.exp(m - m_all)
        num = num + e * o
        den = den + e * l
    mix_ref[N_HEADS:2 * N_HEADS, :] = num / den

    hc = zc(5) * _sigmoid(zc(6))
    y = jnp.sum(cw_ref[0:CONV_C - 1] * pc_ref[...], axis=0) + cw_ref[CONV_C - 1] * hc + cb_ref[...]
    y = _layer_norm_tile(y, clg_ref[...], clb_ref[...])
    mix_ref[2 * N_HEADS:3 * N_HEADS, :] = y * _sigmoid(y)
    sc_ref[0:CONV_C - 2] = pc_ref[1:CONV_C - 1]
    sc_ref[CONV_C - 2] = hc

    ch = zc(8) * zc(9)
    conv = dw_ref[0] * pd_ref[0] + dw_ref[1] * pd_ref[1] + dw_ref[2] * ch
    mix_ref[3 * N_HEADS:4 * N_HEADS, :] = zc(7) * conv
    sd_ref[0] = pd_ref[1]
    sd_ref[1] = ch


def _sample_mixers(slopes, z_s, cache_k, cache_v, state_c, state_d, layer,
                   a_ln_g, a_ln_b, ws0, bs0, c_conv_w, c_conv_b, c_ln_g, c_ln_b, d_conv_w):
    tile = (N_HEADS, HEAD_DIM)
    vec = pl.BlockSpec((None,) + tile, lambda b: (layer, 0, 0))

    def taps(rows):
        return pl.BlockSpec((None, rows) + tile, lambda b: (layer, 0, 0, 0))

    def cache_spec(rows):
        return pl.BlockSpec((None, None, rows) + tile, lambda b: (layer, b, 0, 0, 0))

    def per_seq(rows):
        return pl.BlockSpec((None, rows) + tile, lambda b: (b, 0, 0, 0))

    def flat(rows):
        return pl.BlockSpec((None, rows, HEAD_DIM), lambda b: (b, 0, 0))

    return pl.pallas_call(
        _sample_body,
        grid=(DEC_BATCH,),
        in_specs=[flat(D_IN // HEAD_DIM), pl.BlockSpec((N_HEADS, 1), lambda b: (0, 0)),
                  cache_spec(WIN), cache_spec(WIN), cache_spec(CONV_C - 1), cache_spec(CONV_D - 1),
                  vec, vec, vec, vec, taps(CONV_C), vec, vec, vec, taps(CONV_D)],
        out_specs=[flat(D_MODEL // HEAD_DIM), per_seq(CONV_C - 1), per_seq(CONV_D - 1), flat(N_HEADS)],
        out_shape=[jax.ShapeDtypeStruct((DEC_BATCH, D_MODEL // HEAD_DIM, HEAD_DIM), F32),
                   jax.ShapeDtypeStruct((DEC_BATCH, CONV_C - 1) + tile, F32),
                   jax.ShapeDtypeStruct((DEC_BATCH, CONV_D - 1) + tile, F32),
                   jax.ShapeDtypeStruct((DEC_BATCH,) + tile, F32)],
        compiler_params=_params(("arbitrary",), 48),
        name="sample_mixers",
    )(z_s, slopes, cache_k, cache_v, state_c, state_d, a_ln_g, a_ln_b, ws0, bs0,
      c_conv_w, c_conv_b, c_ln_g, c_ln_b, d_conv_w)


def _ln_body(x_ref, f_ref, g_ref, b_ref, of_ref, ob_ref):
    y = _layer_norm_rows(ALPHA * x_ref[...] + f_ref[...], g_ref[...], b_ref[...])
    of_ref[...] = y
    ob_ref[...] = y.astype(BF16)


def _route(y, rh_ref, rl_ref):
    yh = y.astype(BF16)
    yl = (y - yh.astype(F32)).astype(BF16)
    logits = (jnp.dot(yh, rh_ref[...], preferred_element_type=F32)
              + jnp.dot(yh, rl_ref[...], preferred_element_type=F32)
              + jnp.dot(yl, rh_ref[...], preferred_element_type=F32))
    lane = lax.broadcasted_iota(jnp.int32, logits.shape, 1)
    lane_f = lane.astype(F32)
    logits = jnp.where(lane < N_EXPERTS, logits, NEG)
    t1 = jnp.max(logits, axis=-1, keepdims=True)
    i1 = jnp.min(jnp.where(logits == t1, lane_f, 128.0), axis=-1, keepdims=True)
    rest = jnp.where(lane_f == i1, NEG, logits)
    t2 = jnp.max(rest, axis=-1, keepdims=True)
    i2 = jnp.min(jnp.where(rest == t2, lane_f, 128.0), axis=-1, keepdims=True)
    e2 = jnp.exp(t2 - t1)
    g1 = 1.0 / (1.0 + e2)
    g2 = e2 / (1.0 + e2)
    out = jnp.where(lane == 0, i1, 0.0)
    out = jnp.where(lane == 1, i2, out)
    out = jnp.where(lane == 2, g1, out)
    return jnp.where(lane == 3, g2, out)


def _ln_route_body(x_ref, f_ref, g_ref, b_ref, rh_ref, rl_ref, of_ref, ob_ref, rt_ref):
    y = _layer_norm_rows(ALPHA * x_ref[...] + f_ref[...], g_ref[...], b_ref[...])
    of_ref[...] = y
    ob_ref[...] = y.astype(BF16)
    rt_ref[...] = _route(y, rh_ref, rl_ref)


def _ln(x, f, g, b, layer, router=None):
    row = pl.BlockSpec((BM_LN, D_MODEL), lambda i: (i, 0))
    vec = pl.BlockSpec((None, 1, D_MODEL), lambda i: (layer, 0, 0))
    in_specs = [row, row, vec, vec]
    out_specs = [row, row]
    out_shape = [jax.ShapeDtypeStruct((T_PAD, D_MODEL), F32), jax.ShapeDtypeStruct((T_PAD, D_MODEL), BF16)]
    args = [x, f, g, b]
    body = _ln_body
    if router is not None:
        rspec = pl.BlockSpec((D_MODEL, 128), lambda i: (0, 0))
        in_specs += [rspec, rspec]
        out_specs.append(pl.BlockSpec((BM_LN, 128), lambda i: (i, 0)))
        out_shape.append(jax.ShapeDtypeStruct((T_PAD, 128), F32))
        args += list(router)
        body = _ln_route_body
    return pl.pallas_call(
        body, grid=(T_PAD // BM_LN,), in_specs=in_specs, out_specs=out_specs, out_shape=out_shape,
        compiler_params=_params(("arbitrary",), 40), name="residual_ln",
    )(*args)


def _ff_up_body(te_ref, tr_ref, x_ref, w1_ref, w3_ref, h_ref, *, bm, sb):
    nblk = (tr_ref[pl.program_id(0)] + sb - 1) // sb
    for nb in range(bm // sb + 1):
        m = nb * sb

        @pl.when(nblk == nb)
        def _():
            if m:
                xs = x_ref[0:m, :]
                a = jnp.dot(xs, w1_ref[...].astype(BF16), preferred_element_type=F32)
                g = jnp.dot(xs, w3_ref[...].astype(BF16), preferred_element_type=F32)
                h_ref[0:m, :] = (a * _sigmoid(a) * g).astype(BF16)
            if m < bm:
                h_ref[m:bm, :] = jnp.zeros((bm - m, h_ref.shape[1]), BF16)


def _ff_up(x_bf, w1, w3, layer, tile_expert, tile_rows, bm, sb):
    s_rows = x_bf.shape[0]
    f = w1.shape[-1]
    nj = f // FF_BN

    def wmap(i, j, te, tr):
        used = (tr[i] > 0).astype(jnp.int32)
        return (layer, te[i], 0, j * used + (nj - 1) * (1 - used))

    wspec = pl.BlockSpec((None, None, D_MODEL, FF_BN), wmap)
    return pl.pallas_call(
        functools.partial(_ff_up_body, bm=bm, sb=sb),
        grid_spec=pltpu.PrefetchScalarGridSpec(
            num_scalar_prefetch=2,
            grid=(s_rows // bm, nj),
            in_specs=[pl.BlockSpec((bm, D_MODEL), lambda i, j, te, tr: (i, 0)), wspec, wspec],
            out_specs=pl.BlockSpec((bm, FF_BN), lambda i, j, te, tr: (i, j))),
        out_shape=jax.ShapeDtypeStruct((s_rows, f), BF16),
        compiler_params=_params(("arbitrary", "arbitrary"), 52),
        name="ff_up",
    )(tile_expert, tile_rows, x_bf, w1, w3)


def _ff_down_body(te_ref, tr_ref, h_ref, w2_ref, o_ref, *, bm, sb, f):
    nblk = (tr_ref[pl.program_id(0)] + sb - 1) // sb
    k = pl.program_id(2)
    valid = f - k * FF_TK

    @pl.when(k == 0)
    def _():
        o_ref[...] = jnp.zeros(o_ref.shape, F32)

    for nb in range(1, bm // sb + 1):
        m = nb * sb

        @pl.when(nblk == nb)
        def _():
            hs = h_ref[0:m, :]
            w = w2_ref[...]
            if f % FF_TK:
                cid = lax.broadcasted_iota(jnp.int32, hs.shape, 1)
                hs = jnp.where(cid < valid, hs, jnp.zeros_like(hs))
                rid = lax.broadcasted_iota(jnp.int32, w.shape, 0)
                w = jnp.where(rid < valid, w, 0.0)
            o_ref[0:m, :] += jnp.dot(hs, w.astype(BF16), preferred_element_type=F32)


def _ff_down(h, w2, layer, tile_expert, tile_rows, bm, sb):
    s_rows, f = h.shape
    nk = pl.cdiv(f, FF_TK)

    def wmap(i, j, k, te, tr):
        used = (tr[i] > 0).astype(jnp.int32)
        return (layer, te[i], k * used + (nk - 1) * (1 - used), j * used + (1 - used))

    return pl.pallas_call(
        functools.partial(_ff_down_body, bm=bm, sb=sb, f=f),
        grid_spec=pltpu.PrefetchScalarGridSpec(
            num_scalar_prefetch=2,
            grid=(s_rows // bm, D_MODEL // FF_BN_DOWN, nk),
            in_specs=[pl.BlockSpec((bm, FF_TK), lambda i, j, k, te, tr: (i, k)),
                      pl.BlockSpec((None, None, FF_TK, FF_BN_DOWN), wmap)],
            out_specs=pl.BlockSpec((bm, FF_BN_DOWN), lambda i, j, k, te, tr: (i, j))),
        out_shape=jax.ShapeDtypeStruct((s_rows, D_MODEL), F32),
        compiler_params=_params(("arbitrary", "arbitrary", "arbitrary"), 52),
        name="ff_down",
    )(tile_expert, tile_rows, h, w2)


def _row_copy(src_hbm, row, buf, r, sem):
    return pltpu.make_async_copy(src_hbm.at[pl.ds(row, 1), :], buf.at[pl.ds(r, 1), :], sem)


def _gather_into(bufs, src_hbm, row_of, nvalid, sem):
    n_rows = bufs[0].shape[0]

    def start(r, c):
        for b, buf in enumerate(bufs):
            _row_copy(src_hbm, row_of(b, r), buf, r, sem).start()
        return c

    @pl.when(nvalid == n_rows)
    def _():
        lax.fori_loop(0, n_rows, start, 0, unroll=8)
        for buf in bufs:
            pltpu.make_async_copy(src_hbm.at[pl.ds(0, n_rows), :], buf, sem).wait()

    @pl.when(nvalid < n_rows)
    def _():
        def wait(r, c):
            for buf in bufs:
                _row_copy(src_hbm, 0, buf, r, sem).wait()
            return c

        lax.fori_loop(0, nvalid, start, 0)
        lax.fori_loop(0, nvalid, wait, 0)


def _gather_body(src_ref, tr_ref, x_hbm, o_ref, buf, sem):
    base = pl.program_id(0) * R_GATHER
    tile = base // BM_MOE
    nvalid = jnp.clip(tr_ref[tile] - (base - tile * BM_MOE), 0, R_GATHER)
    _gather_into([buf], x_hbm, lambda b, r: src_ref[base + r], nvalid, sem)
    rid = lax.broadcasted_iota(jnp.int32, (R_GATHER, 1), 0)
    o_ref[...] = jnp.where(rid < nvalid, buf[...], 0.0).astype(BF16)


def _gather_rows(x_f32, src_rows, tile_rows):
    return pl.pallas_call(
        _gather_body,
        grid_spec=pltpu.PrefetchScalarGridSpec(
            num_scalar_prefetch=2,
            grid=(S_PAD // R_GATHER,),
            in_specs=[pl.BlockSpec(memory_space=pl.ANY)],
            out_specs=pl.BlockSpec((R_GATHER, D_MODEL), lambda i, s, t: (i, 0)),
            scratch_shapes=[pltpu.VMEM((R_GATHER, D_MODEL), F32), pltpu.SemaphoreType.DMA(())]),
        out_shape=jax.ShapeDtypeStruct((S_PAD, D_MODEL), BF16),
        compiler_params=_params(("arbitrary",), 32),
        name="moe_gather",
    )(src_rows, tile_rows, x_f32)


def _combine_body(pos_ref, x_ref, gate_ref, g_ref, b_ref, y_hbm, o0_ref, o1_ref, buf0, buf1, sem, *, final):
    base = pl.program_id(0) * R_COMBINE
    nvalid = jnp.clip(T_REAL - base, 0, R_COMBINE)
    _gather_into([buf0, buf1], y_hbm, lambda b, r: pos_ref[TOP_K * (base + r) + b], nvalid, sem)
    rid = lax.broadcasted_iota(jnp.int32, (R_COMBINE, 1), 0)
    ffn = jnp.where(rid < nvalid, gate_ref[:, 0:1] * buf0[...] + gate_ref[:, 1:2] * buf1[...], 0.0)
    y = _layer_norm_rows(ALPHA * x_ref[...] + ffn, g_ref[...], b_ref[...])
    if final:
        @pl.when(base < T_PROMPT)
        def _():
            o0_ref[...] = y

        @pl.when(base >= T_PROMPT)
        def _():
            o1_ref[...] = y
    else:
        o0_ref[...] = y
        o1_ref[...] = y.astype(BF16)


def _combine_ln(x, gates, g, b, layer, y_slots, pos, final):
    row = pl.BlockSpec((R_COMBINE, D_MODEL), lambda i, p: (i, 0))
    vec = pl.BlockSpec((None, 1, D_MODEL), lambda i, p: (layer, 0, 0))
    if final:
        n_prompt = T_PROMPT // R_COMBINE
        out_specs = [pl.BlockSpec((R_COMBINE, D_MODEL), lambda i, p: (jnp.minimum(i, n_prompt - 1), 0)),
                     pl.BlockSpec((R_COMBINE, D_MODEL), lambda i, p: (jnp.maximum(i - n_prompt, 0), 0))]
        out_shape = [jax.ShapeDtypeStruct((T_PROMPT, D_MODEL), F32),
                     jax.ShapeDtypeStruct((T_PAD - T_PROMPT, D_MODEL), F32)]
    else:
        out_specs = [row, row]
        out_shape = [jax.ShapeDtypeStruct((T_PAD, D_MODEL), F32), jax.ShapeDtypeStruct((T_PAD, D_MODEL), BF16)]
    return pl.pallas_call(
        functools.partial(_combine_body, final=final),
        grid_spec=pltpu.PrefetchScalarGridSpec(
            num_scalar_prefetch=1,
            grid=(T_PAD // R_COMBINE,),
            in_specs=[row, pl.BlockSpec((R_COMBINE, TOP_K), lambda i, p: (i, 0)), vec, vec,
                      pl.BlockSpec(memory_space=pl.ANY)],
            out_specs=out_specs,
            scratch_shapes=[pltpu.VMEM((R_COMBINE, D_MODEL), F32), pltpu.VMEM((R_COMBINE, D_MODEL), F32),
                            pltpu.SemaphoreType.DMA(())]),
        out_shape=out_shape,
        compiler_params=_params(("arbitrary",), 32),
        name="moe_combine_ln",
    )(pos, x, gates, g, b, y_slots)


def _dispatch_plan(route):
    idx = route[:T_REAL, 0:TOP_K].astype(jnp.int32)
    flat = idx.reshape(-1)
    onehot = (flat[:, None] == jnp.arange(N_EXPERTS)[None, :]).astype(jnp.int32)
    counts = jnp.sum(onehot, axis=0)
    rank = jnp.sum((jnp.cumsum(onehot, axis=0) - 1) * onehot, axis=1)
    tiles_e = (counts + BM_MOE - 1) // BM_MOE
    tile_end = jnp.cumsum(tiles_e)
    tile_start = tile_end - tiles_e
    pos = tile_start[flat] * BM_MOE + rank
    tile_id = jnp.arange(MOE_TILES)
    tile_expert = jnp.minimum(jnp.sum((tile_id[:, None] >= tile_end[None, :]).astype(jnp.int32), axis=1),
                              N_EXPERTS - 1)
    last_used = jnp.max(jnp.where(counts > 0, jnp.arange(N_EXPERTS), 0))
    used = tile_id < tile_end[-1]
    tile_expert = jnp.where(used, tile_expert, last_used).astype(jnp.int32)
    tile_rows = jnp.clip(counts[tile_expert] - (tile_id - tile_start[tile_expert]) * BM_MOE, 0, BM_MOE)
    tile_rows = jnp.where(used, tile_rows, 0).astype(jnp.int32)
    src_rows = jnp.zeros((S_PAD,), jnp.int32).at[pos].set(jnp.arange(N_SLOTS, dtype=jnp.int32) // TOP_K)
    gates = jnp.pad(route[:T_REAL, 2:2 + TOP_K], ((0, T_PAD - T_REAL), (0, 0)))
    return tile_expert, tile_rows, src_rows, pos.astype(jnp.int32), gates


def kernel(x_prompt, x_sample, cache_b_k, cache_b_v, state_c, state_d, w_in, a_ln_g, a_ln_b, a_ws, a_bs, c_conv_w, c_conv_b, c_ln_g, c_ln_b, d_conv_w, w_out, ln1_g, ln1_b, ln2_g, ln2_b, ffn_w1, ffn_w3, ffn_w2, moe_router, moe_w1, moe_w3, moe_w2):
    slopes = jnp.asarray([2.0 ** (-8.0 * (h + 1) / N_HEADS) for h in range(N_HEADS)], F32)
    x = jnp.concatenate([x_prompt.reshape(T_PROMPT, D_MODEL), x_sample.reshape(DEC_BATCH, D_MODEL),
                         jnp.zeros((T_PAD - T_REAL, D_MODEL), F32)], axis=0)
    x_bf = x.astype(BF16)

    def vec3(p):
        return p.reshape(DEPTH, 1, -1)

    a_ln_g3, a_ln_b3, c_conv_b3, c_ln_g3, c_ln_b3 = map(vec3, (a_ln_g, a_ln_b, c_conv_b, c_ln_g, c_ln_b))
    ln1_g3, ln1_b3, ln2_g3, ln2_b3 = map(vec3, (ln1_g, ln1_b, ln2_g, ln2_b))
    a_bs_t = jnp.swapaxes(a_bs, 1, 2)

    def tiles(p):
        return p.reshape(p.shape[:-1] + (N_HEADS, HEAD_DIM))

    tile_shape = (DEPTH, N_HEADS, HEAD_DIM)
    ws0 = jnp.broadcast_to(a_ws[:, :, 0, 0][:, :, None], tile_shape)
    bs0 = jnp.broadcast_to(a_bs[:, :, 0][:, :, None], tile_shape)
    sample_params = (tiles(a_ln_g), tiles(a_ln_b), ws0, bs0, tiles(c_conv_w), tiles(c_conv_b),
                     tiles(c_ln_g), tiles(c_ln_b), tiles(d_conv_w))
    state_c_t, state_d_t = tiles(state_c), tiles(state_d)
    slope_col = slopes.reshape(N_HEADS, 1)
    dense_tiles = T_PAD // BM
    dense_expert = jnp.zeros((dense_tiles,), jnp.int32)
    dense_rows = jnp.full((dense_tiles,), BM, jnp.int32)

    assert DEPTH % 2 == 0
    sk, sv, pc, scs, pd, sds, sa = [], [], [], [], [], [], []
    kv = [jnp.zeros((DEPTH, T_PROMPT, D_Q), F32) for _ in range(2)]
    for l in range(DEPTH):
        z = _proj(x_bf, w_in, l)
        yb = _attn_prompt(z, slopes)
        z_s = z[T_PROMPT:T_REAL].reshape(DEC_BATCH, D_IN // HEAD_DIM, HEAD_DIM)
        mix_s, c_buf_s, d_buf_s, v_a_s = _sample_mixers(
            slope_col, z_s, cache_b_k, cache_b_v, state_c_t, state_d_t, l, *sample_params)
        mixcat, c_buf, d_buf, *kv = _mix_acd(z, yb, mix_s.reshape(DEC_BATCH, D_MODEL), l, kv, a_ln_g3, a_ln_b3,
                                             a_ws, a_bs_t, c_conv_w, c_conv_b3, c_ln_g3, c_ln_b3, d_conv_w)
        mix = _proj(mixcat, w_out, l)

        sk.append(z_s[:, 3 * N_HEADS:4 * N_HEADS].reshape(DEC_BATCH, 1, N_HEADS, HEAD_DIM))
        sv.append(z_s[:, 4 * N_HEADS:5 * N_HEADS].reshape(DEC_BATCH, 1, N_HEADS, HEAD_DIM))
        pc.append(c_buf)
        scs.append(c_buf_s.reshape(DEC_BATCH, CONV_C - 1, D_Q))
        pd.append(d_buf)
        sds.append(d_buf_s.reshape(DEC_BATCH, CONV_D - 1, D_Q))
        sa.append(v_a_s.reshape(DEC_BATCH, 1, D_Q))

        i = l // 2
        if l % 2 == 0:
            x1, x1_bf = _ln(x, mix, ln1_g3, ln1_b3, l)
            hid = _ff_up(x1_bf, ffn_w1[:, None], ffn_w3[:, None], i, dense_expert, dense_rows, BM, BM)
            ffn = _ff_down(hid, ffn_w2[:, None], i, dense_expert, dense_rows, BM, BM)
            x, x_bf = _ln(x1, ffn, ln2_g3, ln2_b3, l)
        else:
            r = jnp.pad(moe_router[i], ((0, 0), (0, 128 - N_EXPERTS)))
            r_hi = r.astype(BF16)
            r_lo = (r - r_hi.astype(F32)).astype(BF16)
            x1, x1_bf, route = _ln(x, mix, ln1_g3, ln1_b3, l, router=(r_hi, r_lo))
            tile_expert, tile_rows, src_rows, pos, gates = _dispatch_plan(route)
            xs = _gather_rows(x1, src_rows, tile_rows)
            hid = _ff_up(xs, moe_w1, moe_w3, i, tile_expert, tile_rows, BM_MOE, SB_MOE)
            y_slots = _ff_down(hid, moe_w2, i, tile_expert, tile_rows, BM_MOE, SB_MOE)
            final = l == DEPTH - 1
            x, x_bf = _combine_ln(x1, gates, ln2_g3, ln2_b3, l, y_slots, pos, final)

    y_prompt, y_tail = x, x_bf
    kv_shape = (DEPTH, BATCH, SEQ, N_HEADS, HEAD_DIM)
    return (y_prompt.reshape(BATCH, SEQ, D_MODEL), y_tail[:DEC_BATCH].reshape(DEC_BATCH, 1, D_MODEL),
            kv[0].reshape(kv_shape), kv[1].reshape(kv_shape), jnp.stack(sk), jnp.stack(sv),
            jnp.stack(pc), jnp.stack(scs), jnp.stack(pd), jnp.stack(sds), jnp.stack(sa))
```

```python
import functools

import jax
import jax.numpy as jnp
from jax import lax
from jax.experimental import pallas as pl
from jax.experimental.pallas import tpu as pltpu

D_MODEL = 4096
BATCH = 4
SEQ = 2048
DEPTH = 2
DEC_BATCH = 8
HEAD_DIM = 128
D_Q = D_MODEL // 4
N_HEADS = D_Q // HEAD_DIM
D_IN = 10 * D_Q
CHUNK = 128
DILATIONS = (1, 4, 16)
N_BACK = 128
WIN = 2048
CONV_C = 31
CONV_D = 3
D_FF = 11008
N_EXPERTS = 8
TOP_K = 2
D_FF_EXPERT = 14336
ALPHA = (2.0 * DEPTH) ** 0.25
LN_EPS = 1e-5
ATTN_SCALE = HEAD_DIM ** -0.5
NEG = -1e30

T_PROMPT = BATCH * SEQ
T_REAL = T_PROMPT + DEC_BATCH
T_PAD = 8448
BM = 1056
BN_PROJ = 512
BT_MIX = 256
BM_LN = 256
FF_BN = 256
FF_TK = 1024
FF_BN_DOWN = 2048
BM_MOE = 1280
SB_MOE = 128
N_SLOTS = T_REAL * TOP_K
MOE_TILES = (N_SLOTS + N_EXPERTS * (BM_MOE - 1)) // BM_MOE
S_PAD = MOE_TILES * BM_MOE
R_GATHER = 256
R_COMBINE = 128
V7X_VMEM_BYTES = 64 * 1024 * 1024

F32 = jnp.float32
BF16 = jnp.bfloat16


def _params(sem, vmem_mb):
    assert vmem_mb * 1024 * 1024 < V7X_VMEM_BYTES
    return pltpu.CompilerParams(dimension_semantics=sem, vmem_limit_bytes=vmem_mb * 1024 * 1024)


def _sigmoid(x):
    return 1.0 / (1.0 + jnp.exp(-x))


def _layer_norm_rows(y, g, b):
    mu = jnp.mean(y, axis=-1, keepdims=True)
    yc = y - mu
    var = jnp.mean(yc * yc, axis=-1, keepdims=True)
    return yc * lax.rsqrt(var + LN_EPS) * g + b


def _cast_rows(dst_ref, src_ref, rows, chunk=256):
    def body(i, c):
        sl = pl.ds(pl.multiple_of(i * chunk, chunk), chunk)
        dst_ref[sl, :] = src_ref[sl, :].astype(BF16)
        return c
    lax.fori_loop(0, rows // chunk, body, 0)


def _proj_body(x_ref, w_ref, o_ref, wb_ref):
    @pl.when(pl.program_id(1) == 0)
    def _():
        _cast_rows(wb_ref, w_ref, wb_ref.shape[0])
    o_ref[...] = jnp.dot(x_ref[...], wb_ref[...], preferred_element_type=F32)


def _proj(x_bf, w_stack, layer):
    m, k = x_bf.shape
    n = w_stack.shape[-1]
    return pl.pallas_call(
        _proj_body,
        grid=(n // BN_PROJ, m // BM),
        in_specs=[pl.BlockSpec((BM, k), lambda j, i: (i, 0)),
                  pl.BlockSpec((None, k, BN_PROJ), lambda j, i: (layer, 0, j))],
        out_specs=pl.BlockSpec((BM, BN_PROJ), lambda j, i: (i, j)),
        out_shape=jax.ShapeDtypeStruct((m, n), F32),
        scratch_shapes=[pltpu.VMEM((k, BN_PROJ), BF16)],
        compiler_params=_params(("arbitrary", "arbitrary"), 48),
        name="proj",
    )(x_bf, w_stack)


def _mix_acd_body(up_ref, vp_ref, zk_ref, zv_ref, ca_ref, cg_ref, dbg_ref, dcg_ref, dh_ref, yb_ref, mixs_ref,
                  lng_ref, lnb_ref, ws_ref, bst_ref, cw_ref, cb_ref, clg_ref, clb_ref, dw_ref,
                  pk_in_ref, pv_in_ref,
                  mix_ref, sc_ref, sd_ref, pk_ref, pv_ref, vbuf, hext, chext, ybuf):
    del pk_in_ref, pv_in_ref
    s = pl.program_id(0)
    nt = SEQ // BT_MIX
    t = s % nt
    bt = BT_MIX

    @pl.when(s == BATCH * nt)
    def _():
        pad = jnp.zeros((bt - DEC_BATCH, D_MODEL), F32)
        mix_ref[...] = jnp.concatenate([mixs_ref[...], pad], axis=0).astype(BF16)

    @pl.when(s < BATCH * nt)
    def _():
        @pl.when(t == 0)
        def _():
            hext[0:32, :] = jnp.zeros((32, D_Q), F32)
            chext[0:8, :] = jnp.zeros((8, D_Q), F32)

        pk_ref[...] = zk_ref[...]
        pv_ref[...] = zv_ref[...]
        mix_ref[:, D_Q:2 * D_Q] = yb_ref[...]

        v = jax.nn.gelu(vp_ref[...])
        vbuf[...] = _layer_norm_rows(v, lng_ref[...], lnb_ref[...]).astype(BF16)
        row = lax.broadcasted_iota(jnp.int32, (CHUNK, CHUNK), 0)
        col = lax.broadcasted_iota(jnp.int32, (CHUNK, CHUNK), 1)
        for h in range(N_HEADS):
            cs = slice(h * HEAD_DIM, (h + 1) * HEAD_DIM)
            w = jnp.where(row >= col, ws_ref[h], 0.0).astype(BF16)
            bias = bst_ref[:, h:h + 1]
            for c in range(bt // CHUNK):
                rs = slice(c * CHUNK, (c + 1) * CHUNK)
                mixed = jnp.dot(w, vbuf[rs, cs], preferred_element_type=F32) + bias
                mix_ref[rs, cs] = (jax.nn.gelu(up_ref[rs, cs]) * mixed).astype(BF16)

        hext[32:32 + bt, :] = ca_ref[...] * _sigmoid(cg_ref[...])
        rc = 64
        for cc in range(D_Q // 128):
            cs = slice(cc * 128, (cc + 1) * 128)
            for r0 in range(0, bt, rc):
                acc = jnp.zeros((rc, 128), F32) + cb_ref[:, cs]
                for k in range(CONV_C):
                    acc = acc + cw_ref[k:k + 1, cs] * hext[r0 + 2 + k:r0 + 2 + k + rc, cs]
                ybuf[r0:r0 + rc, cs] = acc
        y = _layer_norm_rows(ybuf[...], clg_ref[...], clb_ref[...])
        mix_ref[:, 2 * D_Q:3 * D_Q] = (y * _sigmoid(y)).astype(BF16)

        @pl.when(t == nt - 1)
        def _():
            sc_ref[...] = hext[bt + 2:bt + 32, :]
        hext[0:32, :] = hext[bt:bt + 32, :]

        chext[8:8 + bt, :] = dcg_ref[...] * dh_ref[...]
        conv = (dw_ref[0:1, :] * chext[6:6 + bt, :] + dw_ref[1:2, :] * chext[7:7 + bt, :]
                + dw_ref[2:3, :] * chext[8:8 + bt, :])
        mix_ref[:, 3 * D_Q:4 * D_Q] = (dbg_ref[...] * conv).astype(BF16)

        @pl.when(t == nt - 1)
        def _():
            sd_ref[...] = chext[bt + 6:bt + 8, :]
        chext[0:8, :] = chext[bt:bt + 8, :]


def _mix_acd(z, yb, mix_s, layer, kv_prev, a_ln_g, a_ln_b, a_ws, a_bs_t, c_conv_w, c_conv_b, c_ln_g,
             c_ln_b, d_conv_w):
    nt = SEQ // BT_MIX
    last = BATCH * nt - 1

    def zcol(c):
        return pl.BlockSpec((BT_MIX, D_Q), lambda s: (s, c))

    def vec(rows):
        return pl.BlockSpec((None, rows, D_Q), lambda s: (layer, 0, 0))

    def state(rows):
        return pl.BlockSpec((None, rows, D_Q), lambda s: (jnp.minimum(s // nt, BATCH - 1), 0, 0))

    kv_spec = pl.BlockSpec((None, BT_MIX, D_Q), lambda s: (layer, jnp.minimum(s, last), 0))
    kv_shape = jax.ShapeDtypeStruct((DEPTH, T_PROMPT, D_Q), F32)
    in_specs = [zcol(0), zcol(1), zcol(3), zcol(4), zcol(5), zcol(6), zcol(7), zcol(8), zcol(9),
                pl.BlockSpec((BT_MIX, D_Q), lambda s: (jnp.minimum(s, last), 0)),
                pl.BlockSpec((DEC_BATCH, D_MODEL), lambda s: (0, 0)),
                vec(1), vec(1),
                pl.BlockSpec((None, N_HEADS, CHUNK, CHUNK), lambda s: (layer, 0, 0, 0)),
                pl.BlockSpec((None, CHUNK, N_HEADS), lambda s: (layer, 0, 0)),
                vec(CONV_C), vec(1), vec(1), vec(1), vec(CONV_D),
                pl.BlockSpec(memory_space=pl.ANY), pl.BlockSpec(memory_space=pl.ANY)]
    args = [z] * 9 + [yb, mix_s, a_ln_g, a_ln_b, a_ws, a_bs_t, c_conv_w, c_conv_b, c_ln_g, c_ln_b, d_conv_w,
                      *kv_prev]
    aliases = {len(args) - 2: 3, len(args) - 1: 4}
    return pl.pallas_call(
        _mix_acd_body,
        grid=(T_PAD // BT_MIX,),
        in_specs=in_specs,
        out_specs=[pl.BlockSpec((BT_MIX, D_MODEL), lambda s: (s, 0)), state(CONV_C - 1), state(CONV_D - 1),
                   kv_spec, kv_spec],
        out_shape=[jax.ShapeDtypeStruct((T_PAD, D_MODEL), BF16),
                   jax.ShapeDtypeStruct((BATCH, CONV_C - 1, D_Q), F32),
                   jax.ShapeDtypeStruct((BATCH, CONV_D - 1, D_Q), F32), kv_shape, kv_shape],
        scratch_shapes=[pltpu.VMEM((BT_MIX, D_Q), BF16),
                        pltpu.VMEM((BT_MIX + 32, D_Q), F32),
                        pltpu.VMEM((BT_MIX + 8, D_Q), F32),
                        pltpu.VMEM((BT_MIX, D_Q), F32)],
        input_output_aliases=aliases,
        compiler_params=_params(("arbitrary",), 48),
        name="mix_acd",
    )(*args)


def _attn_body(slopes_ref, q_ref, k_ref, v_ref, o_ref, qd, kd, vd, od, md, ld, ob, mb, lb):
    slope = slopes_ref[pl.program_id(1)]
    qi = lax.broadcasted_iota(jnp.int32, (128, 256), 0)
    ci = lax.broadcasted_iota(jnp.int32, (128, 256), 1)
    dist = qi + 128 - ci
    in_band = jnp.logical_and(dist >= 0, dist <= N_BACK)
    nblk = SEQ // 128
    kd[0:128, :] = jnp.zeros((128, HEAD_DIM), BF16)
    vd[0:128, 0:HEAD_DIM] = jnp.zeros((128, HEAD_DIM), BF16)
    vd[:, HEAD_DIM:2 * HEAD_DIM] = jnp.ones((SEQ + 128, HEAD_DIM), BF16)

    for g, d in enumerate(DILATIONS):
        sub = SEQ // d
        for r in range(d):
            src = pl.ds(r, sub, stride=d) if d > 1 else slice(0, SEQ)
            qd[r * sub:(r + 1) * sub, :] = (q_ref[src, :] * ATTN_SCALE).astype(BF16)
            kd[128 + r * sub:128 + (r + 1) * sub, :] = k_ref[src, :].astype(BF16)
            vd[128 + r * sub:128 + (r + 1) * sub, 0:HEAD_DIM] = v_ref[src, :].astype(BF16)
        per_sub = sub // 128
        bias = jnp.where(in_band, (-slope * float(d)) * dist.astype(F32), NEG)

        def block(i, carry):
            off = pl.multiple_of(i * 128, 128)
            first_col = jnp.where((i % per_sub) == 0, 128, 0)
            s = lax.dot_general(qd[pl.ds(off, 128), :], kd[pl.ds(off, 256), :], (((1,), (1,)), ((), ())),
                                preferred_element_type=F32)
            s = jnp.where(ci >= first_col, s + bias, NEG)
            m = jnp.max(s, axis=-1, keepdims=True)
            p = jnp.exp(s - m).astype(BF16)
            acc = jnp.dot(p, vd[pl.ds(off, 256), :], preferred_element_type=F32)
            od[pl.ds(off, 128), :] = acc[:, 0:HEAD_DIM]
            ld[pl.ds(off, 128), :] = acc[:, HEAD_DIM:2 * HEAD_DIM]
            md[pl.ds(off, 128), :] = jnp.broadcast_to(m, (128, HEAD_DIM))
            return carry

        lax.fori_loop(0, nblk, block, 0, unroll=8)
        for r in range(d):
            dst = pl.ds(r, sub, stride=d) if d > 1 else slice(0, SEQ)
            src = slice(r * sub, (r + 1) * sub)
            ob[g, dst, :] = od[src, :]
            mb[g, dst, :] = md[src, :]
            lb[g, dst, :] = ld[src, :]

    m_all = jnp.maximum(jnp.maximum(mb[0], mb[1]), mb[2])
    num = jnp.zeros((SEQ, HEAD_DIM), F32)
    den = jnp.zeros((SEQ, HEAD_DIM), F32)
    for g in range(len(DILATIONS)):
        e = jnp.exp(mb[g] - m_all)
        num = num + e * ob[g]
        den = den + e * lb[g]
    o_ref[...] = (num / den).astype(BF16)


def _attn_prompt(z, slopes):
    big = pltpu.VMEM((SEQ, HEAD_DIM), F32)
    small = pltpu.VMEM((SEQ, HEAD_DIM), BF16)
    tri = pltpu.VMEM((len(DILATIONS), SEQ, HEAD_DIM), F32)
    col0 = 2 * N_HEADS
    return pl.pallas_call(
        _attn_body,
        grid_spec=pltpu.PrefetchScalarGridSpec(
            num_scalar_prefetch=1,
            grid=(BATCH, N_HEADS),
            in_specs=[pl.BlockSpec((SEQ, HEAD_DIM), lambda b, h, s: (b, col0 + h)),
                      pl.BlockSpec((SEQ, HEAD_DIM), lambda b, h, s: (b, col0 + N_HEADS + h)),
                      pl.BlockSpec((SEQ, HEAD_DIM), lambda b, h, s: (b, col0 + 2 * N_HEADS + h))],
            out_specs=pl.BlockSpec((SEQ, HEAD_DIM), lambda b, h, s: (b, h)),
            scratch_shapes=[small, pltpu.VMEM((SEQ + 128, HEAD_DIM), BF16),
                            pltpu.VMEM((SEQ + 128, 2 * HEAD_DIM), BF16), big, big, big, tri, tri, tri]),
        out_shape=jax.ShapeDtypeStruct((T_PROMPT, D_Q), BF16),
        compiler_params=_params(("arbitrary", "arbitrary"), 40),
        name="attn_prompt",
    )(slopes, z, z, z)


def _layer_norm_tile(y, g, b):
    mu = jnp.mean(y, axis=(0, 1), keepdims=True)
    yc = y - mu
    var = jnp.mean(yc * yc, axis=(0, 1), keepdims=True)
    return yc * lax.rsqrt(var + LN_EPS) * g + b


def _sample_body(z_ref, slope_ref, k1_ref, k4_ref, k16_ref, v1_ref, v4_ref, v16_ref, pc_ref, pd_ref,
                 lng_ref, lnb_ref, ws0_ref, bs0_ref, cw_ref, cb_ref, clg_ref, clb_ref, dw_ref,
                 mix_ref, sc_ref, sd_ref, va_ref):
    def zc(c):
        return z_ref[c * N_HEADS:(c + 1) * N_HEADS, :]

    u = jax.nn.gelu(zc(0))
    v = _layer_norm_tile(jax.nn.gelu(zc(1)), lng_ref[...], lnb_ref[...])
    va_ref[...] = v
    mix_ref[0:N_HEADS, :] = u * (ws0_ref[...] * v + bs0_ref[...])

    q = zc(2) * ATTN_SCALE
    k_new = zc(3)
    v_new = zc(4)
    s_new = jnp.sum(q * k_new, axis=-1, keepdims=True)
    jdist = (N_BACK - lax.broadcasted_iota(jnp.int32, (N_BACK, 1, 1), 0)).astype(F32)
    ms, ls, outs = [], [], []
    for d, kc_ref, vc_ref in zip(DILATIONS, (k1_ref, k4_ref, k16_ref), (v1_ref, v4_ref, v16_ref)):
        s = jnp.sum(kc_ref[...] * q[None], axis=-1, keepdims=True)
        s = s - (float(d) * slope_ref[...])[None] * jdist
        m = jnp.maximum(jnp.max(s, axis=0), s_new)
        p = jnp.exp(s - m[None])
        p_new = jnp.exp(s_new - m)
        ls.append(jnp.sum(p, axis=0) + p_new)
        outs.append(jnp.sum(p * vc_ref[...], axis=0) + p_new * v_new)
        ms.append(m)
    m_all = jnp.maximum(jnp.maximum(ms[0], ms[1]), ms[2])
    num = jnp.zeros((N_HEADS, HEAD_DIM), F32)
    den = jnp.zeros((N_HEADS, 1), F32)
    for m, l, o in zip(ms, ls, outs):
        e = jnp.exp(m - m_all)
        num = num + e * o
        den = den + e * l
    mix_ref[N_HEADS:2 * N_HEADS, :] = num / den

    hc = zc(5) * _sigmoid(zc(6))
    y = jnp.sum(cw_ref[0:CONV_C - 1] * pc_ref[...], axis=0) + cw_ref[CONV_C - 1] * hc + cb_ref[...]
    y = _layer_norm_tile(y, clg_ref[...], clb_ref[...])
    mix_ref[2 * N_HEADS:3 * N_HEADS, :] = y * _sigmoid(y)
    sc_ref[0:CONV_C - 2] = pc_ref[1:CONV_C - 1]
    sc_ref[CONV_C - 2] = hc

    ch = zc(8) * zc(9)
    conv = dw_ref[0] * pd_ref[0] + dw_ref[1] * pd_ref[1] + dw_ref[2] * ch
    mix_ref[3 * N_HEADS:4 * N_HEADS, :] = zc(7) * conv
    sd_ref[0] = pd_ref[1]
    sd_ref[1] = ch


def _sample_mixers(slopes, z_s, cache_k, cache_v, state_c, state_d, layer,
                   a_ln_g, a_ln_b, ws0, bs0, c_conv_w, c_conv_b, c_ln_g, c_ln_b, d_conv_w):
    tile = (N_HEADS, HEAD_DIM)
    vec = pl.BlockSpec((None,) + tile, lambda b: (layer, 0, 0))

    def taps(rows):
        return pl.BlockSpec((None, rows) + tile, lambda b: (layer, 0, 0, 0))

    def cache_spec(rows):
        return pl.BlockSpec((None, None, rows) + tile, lambda b: (layer, b, 0, 0, 0))

    def per_seq(rows):
        return pl.BlockSpec((None, rows) + tile, lambda b: (b, 0, 0, 0))

    def flat(rows):
        return pl.BlockSpec((None, rows, HEAD_DIM), lambda b: (b, 0, 0))

    def window(d):
        assert WIN % (N_BACK * d) == 0
        last = WIN // d // N_BACK - 1
        return pl.BlockSpec((None, None, N_BACK, None) + tile, lambda b: (layer, b, last, 0, 0, 0))

    def strided(cache, d):
        return cache.reshape(DEPTH, DEC_BATCH, WIN // d, d, N_HEADS, HEAD_DIM)

    windows = [window(d) for d in DILATIONS]
    return pl.pallas_call(
        _sample_body,
        grid=(DEC_BATCH,),
        in_specs=[flat(D_IN // HEAD_DIM), pl.BlockSpec((N_HEADS, 1), lambda b: (0, 0)),
                  *windows, *windows, cache_spec(CONV_C - 1), cache_spec(CONV_D - 1),
                  vec, vec, vec, vec, taps(CONV_C), vec, vec, vec, taps(CONV_D)],
        out_specs=[flat(D_MODEL // HEAD_DIM), per_seq(CONV_C - 1), per_seq(CONV_D - 1), flat(N_HEADS)],
        out_shape=[jax.ShapeDtypeStruct((DEC_BATCH, D_MODEL // HEAD_DIM, HEAD_DIM), F32),
                   jax.ShapeDtypeStruct((DEC_BATCH, CONV_C - 1) + tile, F32),
                   jax.ShapeDtypeStruct((DEC_BATCH, CONV_D - 1) + tile, F32),
                   jax.ShapeDtypeStruct((DEC_BATCH,) + tile, F32)],
        compiler_params=_params(("arbitrary",), 32),
        name="sample_mixers",
    )(z_s, slopes, *[strided(cache_k, d) for d in DILATIONS], *[strided(cache_v, d) for d in DILATIONS],
      state_c, state_d, a_ln_g, a_ln_b, ws0, bs0, c_conv_w, c_conv_b, c_ln_g, c_ln_b, d_conv_w)


def _ln_body(xp_ref, xt_ref, f_ref, g_ref, b_ref, of_ref, ob_ref):
    is_prompt = pl.program_id(0) < T_PROMPT // BM_LN

    @pl.when(is_prompt)
    def _():
        of_ref[...] = ALPHA * xp_ref[...] + f_ref[...]

    @pl.when(jnp.logical_not(is_prompt))
    def _():
        of_ref[...] = ALPHA * xt_ref[...] + f_ref[...]

    y = _layer_norm_rows(of_ref[...], g_ref[...], b_ref[...])
    of_ref[...] = y
    ob_ref[...] = y.astype(BF16)


def _route(y, rh_ref, rl_ref):
    yh = y.astype(BF16)
    yl = (y - yh.astype(F32)).astype(BF16)
    logits = (jnp.dot(yh, rh_ref[...], preferred_element_type=F32)
              + jnp.dot(yh, rl_ref[...], preferred_element_type=F32)
              + jnp.dot(yl, rh_ref[...], preferred_element_type=F32))
    lane = lax.broadcasted_iota(jnp.int32, logits.shape, 1)
    lane_f = lane.astype(F32)
    logits = jnp.where(lane < N_EXPERTS, logits, NEG)
    t1 = jnp.max(logits, axis=-1, keepdims=True)
    i1 = jnp.min(jnp.where(logits == t1, lane_f, 128.0), axis=-1, keepdims=True)
    rest = jnp.where(lane_f == i1, NEG, logits)
    t2 = jnp.max(rest, axis=-1, keepdims=True)
    i2 = jnp.min(jnp.where(rest == t2, lane_f, 128.0), axis=-1, keepdims=True)
    e2 = jnp.exp(t2 - t1)
    g1 = 1.0 / (1.0 + e2)
    g2 = e2 / (1.0 + e2)
    out = jnp.where(lane == 0, i1, 0.0)
    out = jnp.where(lane == 1, i2, out)
    out = jnp.where(lane == 2, g1, out)
    return jnp.where(lane == 3, g2, out)


def _ln_route_body(xp_ref, xt_ref, f_ref, g_ref, b_ref, rh_ref, rl_ref, of_ref, ob_ref, rt_ref):
    _ln_body(xp_ref, xt_ref, f_ref, g_ref, b_ref, of_ref, ob_ref)
    rt_ref[...] = _route(of_ref[...], rh_ref, rl_ref)


def _ln(x, f, g, b, layer, router=None):
    n_prompt = T_PROMPT // BM_LN
    x_prompt, x_tail = x if isinstance(x, tuple) else (x, x)
    tail0 = 0 if isinstance(x, tuple) else n_prompt
    row = pl.BlockSpec((BM_LN, D_MODEL), lambda i: (i, 0))
    vec = pl.BlockSpec((None, 1, D_MODEL), lambda i: (layer, 0, 0))
    in_specs = [pl.BlockSpec((BM_LN, D_MODEL), lambda i: (jnp.minimum(i, n_prompt - 1), 0)),
                pl.BlockSpec((BM_LN, D_MODEL), lambda i: (tail0 + jnp.maximum(i - n_prompt, 0), 0)),
                row, vec, vec]
    out_specs = [row, row]
    out_shape = [jax.ShapeDtypeStruct((T_PAD, D_MODEL), F32), jax.ShapeDtypeStruct((T_PAD, D_MODEL), BF16)]
    args = [x_prompt, x_tail, f, g, b]
    body = _ln_body
    if router is not None:
        rspec = pl.BlockSpec((D_MODEL, 128), lambda i: (0, 0))
        in_specs += [rspec, rspec]
        out_specs.append(pl.BlockSpec((BM_LN, 128), lambda i: (i, 0)))
        out_shape.append(jax.ShapeDtypeStruct((T_PAD, 128), F32))
        args += list(router)
        body = _ln_route_body
    return pl.pallas_call(
        body, grid=(T_PAD // BM_LN,), in_specs=in_specs, out_specs=out_specs, out_shape=out_shape,
        compiler_params=_params(("arbitrary",), 48), name="residual_ln",
    )(*args)


def _ff_up_body(te_ref, tr_ref, ts_ref, x_ref, w1_ref, w3_ref, h_ref, *, bm, sb):
    del te_ref, ts_ref
    nblk = (tr_ref[pl.program_id(0)] + sb - 1) // sb
    for nb in range(bm // sb + 1):
        m = nb * sb

        @pl.when(nblk == nb)
        def _():
            if m:
                xs = x_ref[0:m, :]
                a = jnp.dot(xs, w1_ref[...].astype(BF16), preferred_element_type=F32)
                g = jnp.dot(xs, w3_ref[...].astype(BF16), preferred_element_type=F32)
                h_ref[0:m, :] = (a * _sigmoid(a) * g).astype(BF16)
            if m < bm:
                h_ref[m:bm, :] = jnp.zeros((bm - m, h_ref.shape[1]), BF16)


def _ff_up(x_bf, w1, w3, layer, tiles, bm, sb):
    s_rows = x_bf.shape[0]
    f = w1.shape[-1]
    nj = f // FF_BN

    def wmap(i, j, te, tr, ts):
        used = (tr[i] > 0).astype(jnp.int32)
        return (layer, te[i], 0, j * used + (nj - 1) * (1 - used))

    wspec = pl.BlockSpec((None, None, D_MODEL, FF_BN), wmap)
    return pl.pallas_call(
        functools.partial(_ff_up_body, bm=bm, sb=sb),
        grid_spec=pltpu.PrefetchScalarGridSpec(
            num_scalar_prefetch=3,
            grid=(s_rows // bm, nj),
            in_specs=[pl.BlockSpec((bm, D_MODEL), lambda i, j, te, tr, ts: (ts[i], 0)), wspec, wspec],
            out_specs=pl.BlockSpec((bm, FF_BN), lambda i, j, te, tr, ts: (i, j))),
        out_shape=jax.ShapeDtypeStruct((s_rows, f), BF16),
        compiler_params=_params(("arbitrary", "arbitrary"), 52),
        name="ff_up",
    )(*tiles, x_bf, w1, w3)


def _ff_down_body(te_ref, tr_ref, ts_ref, h_ref, w2_ref, o_ref, *, bm, sb, f):
    del te_ref, ts_ref
    nblk = (tr_ref[pl.program_id(0)] + sb - 1) // sb
    k = pl.program_id(2)
    valid = f - k * FF_TK

    @pl.when(k == 0)
    def _():
        o_ref[...] = jnp.zeros(o_ref.shape, F32)

    for nb in range(1, bm // sb + 1):
        m = nb * sb

        @pl.when(nblk == nb)
        def _():
            hs = h_ref[0:m, :]
            w = w2_ref[...]
            if f % FF_TK:
                cid = lax.broadcasted_iota(jnp.int32, hs.shape, 1)
                hs = jnp.where(cid < valid, hs, jnp.zeros_like(hs))
                rid = lax.broadcasted_iota(jnp.int32, w.shape, 0)
                w = jnp.where(rid < valid, w, 0.0)
            o_ref[0:m, :] += jnp.dot(hs, w.astype(BF16), preferred_element_type=F32)


def _ff_down(h, w2, layer, tiles, bm, sb):
    s_rows, f = h.shape
    nk = pl.cdiv(f, FF_TK)

    def kblock(i, k, tr):
        used = (tr[i] > 0).astype(jnp.int32)
        return k * used + (nk - 1) * (1 - used)

    def wmap(i, j, k, te, tr, ts):
        used = (tr[i] > 0).astype(jnp.int32)
        return (layer, te[i], kblock(i, k, tr), j * used + (1 - used))

    return pl.pallas_call(
        functools.partial(_ff_down_body, bm=bm, sb=sb, f=f),
        grid_spec=pltpu.PrefetchScalarGridSpec(
            num_scalar_prefetch=3,
            grid=(s_rows // bm, D_MODEL // FF_BN_DOWN, nk),
            in_specs=[pl.BlockSpec((bm, FF_TK), lambda i, j, k, te, tr, ts: (ts[i], kblock(i, k, tr))),
                      pl.BlockSpec((None, None, FF_TK, FF_BN_DOWN), wmap)],
            out_specs=pl.BlockSpec((bm, FF_BN_DOWN), lambda i, j, k, te, tr, ts: (i, j))),
        out_shape=jax.ShapeDtypeStruct((s_rows, D_MODEL), F32),
        compiler_params=_params(("arbitrary", "arbitrary", "arbitrary"), 52),
        name="ff_down",
    )(*tiles, h, w2)


def _row_copy(src_hbm, row, buf, r, sem):
    return pltpu.make_async_copy(src_hbm.at[pl.ds(row, 1), :], buf.at[pl.ds(r, 1), :], sem)


def _gather_into(bufs, src_hbm, row_of, nvalid, sem):
    n_rows = bufs[0].shape[0]

    def start(r, c):
        for b, buf in enumerate(bufs):
            _row_copy(src_hbm, row_of(b, r), buf, r, sem).start()
        return c

    @pl.when(nvalid == n_rows)
    def _():
        lax.fori_loop(0, n_rows, start, 0, unroll=8)
        for buf in bufs:
            pltpu.make_async_copy(src_hbm.at[pl.ds(0, n_rows), :], buf, sem).wait()

    @pl.when(nvalid < n_rows)
    def _():
        def wait(r, c):
            for buf in bufs:
                _row_copy(src_hbm, 0, buf, r, sem).wait()
            return c

        lax.fori_loop(0, nvalid, start, 0)
        lax.fori_loop(0, nvalid, wait, 0)


def _gather_body(src_ref, tr_ref, x_hbm, o_ref, buf, sem):
    base = pl.program_id(0) * R_GATHER
    tile = base // BM_MOE
    nvalid = jnp.clip(tr_ref[tile] - (base - tile * BM_MOE), 0, R_GATHER)
    _gather_into([buf], x_hbm, lambda b, r: src_ref[base + r], nvalid, sem)
    rid = lax.broadcasted_iota(jnp.int32, (R_GATHER, 1), 0)
    o_ref[...] = jnp.where(rid < nvalid, buf[...], 0.0).astype(BF16)


def _gather_rows(x_f32, src_rows, tile_rows):
    return pl.pallas_call(
        _gather_body,
        grid_spec=pltpu.PrefetchScalarGridSpec(
            num_scalar_prefetch=2,
            grid=(S_PAD // R_GATHER,),
            in_specs=[pl.BlockSpec(memory_space=pl.ANY)],
            out_specs=pl.BlockSpec((R_GATHER, D_MODEL), lambda i, s, t: (i, 0)),
            scratch_shapes=[pltpu.VMEM((R_GATHER, D_MODEL), F32), pltpu.SemaphoreType.DMA(())]),
        out_shape=jax.ShapeDtypeStruct((S_PAD, D_MODEL), BF16),
        compiler_params=_params(("arbitrary",), 32),
        name="moe_gather",
    )(src_rows, tile_rows, x_f32)


def _combine_body(pos_ref, x_ref, gate_ref, g_ref, b_ref, y_hbm, o0_ref, o1_ref, buf0, buf1, sem, *, final):
    base = pl.program_id(0) * R_COMBINE
    nvalid = jnp.clip(T_REAL - base, 0, R_COMBINE)
    _gather_into([buf0, buf1], y_hbm, lambda b, r: pos_ref[TOP_K * (base + r) + b], nvalid, sem)
    rid = lax.broadcasted_iota(jnp.int32, (R_COMBINE, 1), 0)
    ffn = jnp.where(rid < nvalid, gate_ref[:, 0:1] * buf0[...] + gate_ref[:, 1:2] * buf1[...], 0.0)
    y = _layer_norm_rows(ALPHA * x_ref[...] + ffn, g_ref[...], b_ref[...])
    if final:
        @pl.when(base < T_PROMPT)
        def _():
            o0_ref[...] = y

        @pl.when(base >= T_PROMPT)
        def _():
            o1_ref[...] = y
    else:
        o0_ref[...] = y
        o1_ref[...] = y.astype(BF16)


def _combine_ln(x, gates, g, b, layer, y_slots, pos, final):
    row = pl.BlockSpec((R_COMBINE, D_MODEL), lambda i, p: (i, 0))
    vec = pl.BlockSpec((None, 1, D_MODEL), lambda i, p: (layer, 0, 0))
    if final:
        n_prompt = T_PROMPT // R_COMBINE
        out_specs = [pl.BlockSpec((R_COMBINE, D_MODEL), lambda i, p: (jnp.minimum(i, n_prompt - 1), 0)),
                     pl.BlockSpec((R_COMBINE, D_MODEL), lambda i, p: (jnp.maximum(i - n_prompt, 0), 0))]
        out_shape = [jax.ShapeDtypeStruct((T_PROMPT, D_MODEL), F32),
                     jax.ShapeDtypeStruct((T_PAD - T_PROMPT, D_MODEL), F32)]
    else:
        out_specs = [row, row]
        out_shape = [jax.ShapeDtypeStruct((T_PAD, D_MODEL), F32), jax.ShapeDtypeStruct((T_PAD, D_MODEL), BF16)]
    return pl.pallas_call(
        functools.partial(_combine_body, final=final),
        grid_spec=pltpu.PrefetchScalarGridSpec(
            num_scalar_prefetch=1,
            grid=(T_PAD // R_COMBINE,),
            in_specs=[row, pl.BlockSpec((R_COMBINE, TOP_K), lambda i, p: (i, 0)), vec, vec,
                      pl.BlockSpec(memory_space=pl.ANY)],
            out_specs=out_specs,
            scratch_shapes=[pltpu.VMEM((R_COMBINE, D_MODEL), F32), pltpu.VMEM((R_COMBINE, D_MODEL), F32),
                            pltpu.SemaphoreType.DMA(())]),
        out_shape=out_shape,
        compiler_params=_params(("arbitrary",), 32),
        name="moe_combine_ln",
    )(pos, x, gates, g, b, y_slots)


def _dispatch_plan(route):
    idx = route[:T_REAL, 0:TOP_K].astype(jnp.int32)
    flat = idx.reshape(-1)
    onehot = (flat[:, None] == jnp.arange(N_EXPERTS)[None, :]).astype(jnp.int32)
    counts = jnp.sum(onehot, axis=0)
    rank = jnp.sum((jnp.cumsum(onehot, axis=0) - 1) * onehot, axis=1)
    tiles_e = (counts + BM_MOE - 1) // BM_MOE
    tile_end = jnp.cumsum(tiles_e)
    tile_start = tile_end - tiles_e
    pos = tile_start[flat] * BM_MOE + rank
    tile_id = jnp.arange(MOE_TILES)
    tile_expert = jnp.minimum(jnp.sum((tile_id[:, None] >= tile_end[None, :]).astype(jnp.int32), axis=1),
                              N_EXPERTS - 1)
    last_used = jnp.max(jnp.where(counts > 0, jnp.arange(N_EXPERTS), 0))
    used = tile_id < tile_end[-1]
    tile_expert = jnp.where(used, tile_expert, last_used).astype(jnp.int32)
    tile_rows = jnp.clip(counts[tile_expert] - (tile_id - tile_start[tile_expert]) * BM_MOE, 0, BM_MOE)
    tile_rows = jnp.where(used, tile_rows, 0).astype(jnp.int32)
    tile_src = jnp.minimum(tile_id, tile_end[-1] - 1).astype(jnp.int32)
    src_rows = jnp.zeros((S_PAD,), jnp.int32).at[pos].set(jnp.arange(N_SLOTS, dtype=jnp.int32) // TOP_K)
    gates = jnp.pad(route[:T_REAL, 2:2 + TOP_K], ((0, T_PAD - T_REAL), (0, 0)))
    return (tile_expert, tile_rows, tile_src), src_rows, pos.astype(jnp.int32), gates


def kernel(x_prompt, x_sample, cache_b_k, cache_b_v, state_c, state_d, w_in, a_ln_g, a_ln_b, a_ws, a_bs, c_conv_w, c_conv_b, c_ln_g, c_ln_b, d_conv_w, w_out, ln1_g, ln1_b, ln2_g, ln2_b, ffn_w1, ffn_w3, ffn_w2, moe_router, moe_w1, moe_w3, moe_w2):
    slopes = jnp.asarray([2.0 ** (-8.0 * (h + 1) / N_HEADS) for h in range(N_HEADS)], F32)
    x_tail = jnp.concatenate([x_sample.reshape(DEC_BATCH, D_MODEL),
                              jnp.zeros((T_PAD - T_REAL, D_MODEL), F32)], axis=0)
    x = (x_prompt.reshape(T_PROMPT, D_MODEL), x_tail)
    x_bf = jnp.concatenate([x[0].astype(BF16), x_tail.astype(BF16)], axis=0)

    def vec3(p):
        return p.reshape(DEPTH, 1, -1)

    a_ln_g3, a_ln_b3, c_conv_b3, c_ln_g3, c_ln_b3 = map(vec3, (a_ln_g, a_ln_b, c_conv_b, c_ln_g, c_ln_b))
    ln1_g3, ln1_b3, ln2_g3, ln2_b3 = map(vec3, (ln1_g, ln1_b, ln2_g, ln2_b))
    a_bs_t = jnp.swapaxes(a_bs, 1, 2)

    def tiles(p):
        return p.reshape(p.shape[:-1] + (N_HEADS, HEAD_DIM))

    tile_shape = (DEPTH, N_HEADS, HEAD_DIM)
    ws0 = jnp.broadcast_to(a_ws[:, :, 0, 0][:, :, None], tile_shape)
    bs0 = jnp.broadcast_to(a_bs[:, :, 0][:, :, None], tile_shape)
    sample_params = (tiles(a_ln_g), tiles(a_ln_b), ws0, bs0, tiles(c_conv_w), tiles(c_conv_b),
                     tiles(c_ln_g), tiles(c_ln_b), tiles(d_conv_w))
    state_c_t, state_d_t = tiles(state_c), tiles(state_d)
    slope_col = slopes.reshape(N_HEADS, 1)
    n_dense = T_PAD // BM
    dense_tiles = (jnp.zeros((n_dense,), jnp.int32), jnp.full((n_dense,), BM, jnp.int32),
                   jnp.arange(n_dense, dtype=jnp.int32))

    assert DEPTH % 2 == 0
    sk, sv, pc, scs, pd, sds, sa = [], [], [], [], [], [], []
    kv = [jnp.zeros((DEPTH, T_PROMPT, D_Q), F32) for _ in range(2)]
    for l in range(DEPTH):
        z = _proj(x_bf, w_in, l)
        yb = _attn_prompt(z, slopes)
        z_s = z[T_PROMPT:T_REAL].reshape(DEC_BATCH, D_IN // HEAD_DIM, HEAD_DIM)
        mix_s, c_buf_s, d_buf_s, v_a_s = _sample_mixers(
            slope_col, z_s, cache_b_k, cache_b_v, state_c_t, state_d_t, l, *sample_params)
        mixcat, c_buf, d_buf, *kv = _mix_acd(z, yb, mix_s.reshape(DEC_BATCH, D_MODEL), l, kv, a_ln_g3, a_ln_b3,
                                             a_ws, a_bs_t, c_conv_w, c_conv_b3, c_ln_g3, c_ln_b3, d_conv_w)
        mix = _proj(mixcat, w_out, l)

        sk.append(z_s[:, 3 * N_HEADS:4 * N_HEADS].reshape(DEC_BATCH, 1, N_HEADS, HEAD_DIM))
        sv.append(z_s[:, 4 * N_HEADS:5 * N_HEADS].reshape(DEC_BATCH, 1, N_HEADS, HEAD_DIM))
        pc.append(c_buf)
        scs.append(c_buf_s.reshape(DEC_BATCH, CONV_C - 1, D_Q))
        pd.append(d_buf)
        sds.append(d_buf_s.reshape(DEC_BATCH, CONV_D - 1, D_Q))
        sa.append(v_a_s.reshape(DEC_BATCH, 1, D_Q))

        i = l // 2
        if l % 2 == 0:
            x1, x1_bf = _ln(x, mix, ln1_g3, ln1_b3, l)
            hid = _ff_up(x1_bf, ffn_w1[:, None], ffn_w3[:, None], i, dense_tiles, BM, BM)
            ffn = _ff_down(hid, ffn_w2[:, None], i, dense_tiles, BM, BM)
            x, x_bf = _ln(x1, ffn, ln2_g3, ln2_b3, l)
        else:
            r = jnp.pad(moe_router[i], ((0, 0), (0, 128 - N_EXPERTS)))
            r_hi = r.astype(BF16)
            r_lo = (r - r_hi.astype(F32)).astype(BF16)
            x1, x1_bf, route = _ln(x, mix, ln1_g3, ln1_b3, l, router=(r_hi, r_lo))
            moe_tiles, src_rows, pos, gates = _dispatch_plan(route)
            xs = _gather_rows(x1, src_rows, moe_tiles[1])
            hid = _ff_up(xs, moe_w1, moe_w3, i, moe_tiles, BM_MOE, SB_MOE)
            y_slots = _ff_down(hid, moe_w2, i, moe_tiles, BM_MOE, SB_MOE)
            final = l == DEPTH - 1
            x, x_bf = _combine_ln(x1, gates, ln2_g3, ln2_b3, l, y_slots, pos, final)

    y_prompt, y_tail = x, x_bf
    kv_shape = (DEPTH, BATCH, SEQ, N_HEADS, HEAD_DIM)
    return (y_prompt.reshape(BATCH, SEQ, D_MODEL), y_tail[:DEC_BATCH].reshape(DEC_BATCH, 1, D_MODEL),
            kv[0].reshape(kv_shape), kv[1].reshape(kv_shape), jnp.stack(sk), jnp.stack(sv),
            jnp.stack(pc), jnp.stack(scs), jnp.stack(pd), jnp.stack(sds), jnp.stack(sa))
```
